```python
import math
import jax
import jax.numpy as jnp
from jax import lax
import numpy as np

D_MODEL = 2048
BATCH = 2
SEQ = 4096
DEPTH = 4

GRID_W = 64
CTX_LEN = 256
N_MIXERS = 4
LAYERS_PER_MIXER = DEPTH // N_MIXERS
NORM_EPS = 1e-6
ROPE_BASE = 10000.0
CONV_W = 4
CONV_PAD = (CONV_W // 2, CONV_W - 1 - CONV_W // 2)

LRU_WIDTH = D_MODEL
LRU_BLOCKS = 8
LRU_BLOCK = LRU_WIDTH // LRU_BLOCKS
LRU_C = 8.0

DN_HEADS = 16
DN_DK = D_MODEL // DN_HEADS
DN_DV = D_MODEL // DN_HEADS
DN_CHUNK = 64
DN_IN = 2 * DN_HEADS * DN_DK + 2 * DN_HEADS * DN_DV + 4 * DN_HEADS

RET_HEADS = 8
RET_DK = D_MODEL // RET_HEADS
RET_DV = 2 * D_MODEL // RET_HEADS
RET_CHUNK = 128
RET_IN = 2 * RET_HEADS * RET_DK + 2 * RET_HEADS * RET_DV

ATT_HEADS = 16
ATT_KV_HEADS = 4
ATT_HD = D_MODEL // ATT_HEADS
ATT_IN = (ATT_HEADS + 2 * ATT_KV_HEADS) * ATT_HD
WINDOW = 128
ATT_BLOCK = 128

N_EXPERTS = 32
N_GROUPS = 8
EXPERTS_PER_GROUP = N_EXPERTS // N_GROUPS
TOP_K = 2
D_EXPERT = 512
MOE_BLOCK = 128

F32 = jnp.float32

kernel_name = 'hybrid_latent_trunk'


def rms_norm(x, g):
    xf = x.astype(F32)
    y = xf * lax.rsqrt(jnp.mean(jnp.square(xf), axis=-1, keepdims=True) + NORM_EPS)
    return (y * g.astype(F32)).astype(x.dtype)


def l2_normalize(x):
    return x * lax.rsqrt(jnp.sum(jnp.square(x), axis=-1, keepdims=True) + NORM_EPS)


def identity(t):
    return t


def flip_seq(t):
    return jnp.flip(t, axis=1)


def centred_dwconv(x, w):
    L = x.shape[1]
    xp = jnp.pad(x, ((0, 0), CONV_PAD, (0, 0)))
    y = xp[:, 0:L] * w[0]
    for tap in range(1, CONV_W):
        y = y + xp[:, tap:tap + L] * w[tap]
    return y


def axial_rope_tables(rows, cols, head_dim):
    quarter = head_dim // 4
    inv_freq = ROPE_BASE ** (-jnp.arange(quarter, dtype=F32) / quarter)
    ang_r = rows.astype(F32)[:, None] * inv_freq
    ang_c = cols.astype(F32)[:, None] * inv_freq
    return (jnp.cos(ang_r), jnp.sin(ang_r), jnp.cos(ang_c), jnp.sin(ang_c))


def apply_axial_rope(x, tables):
    cos_r, sin_r, cos_c, sin_c = (t[:, None, :] for t in tables)
    hd = x.shape[-1]
    q4 = hd // 4
    xf = x.astype(F32)

    def rot(v, cos, sin):
        v1, v2 = v[..., :q4], v[..., q4:]
        return jnp.concatenate([v1 * cos - v2 * sin, v2 * cos + v1 * sin], axis=-1)

    out = jnp.concatenate([rot(xf[..., :hd // 2], cos_r, sin_r), rot(xf[..., hd // 2:], cos_c, sin_c)], axis=-1)
    return out.astype(x.dtype)


def to_chunks(t, size):
    Bb, L = t.shape[:2]
    t = t.reshape(Bb, L // size, size, *t.shape[2:])
    return jnp.moveaxis(t, (1, 3), (0, 2))


def from_chunks(t):
    t = jnp.moveaxis(t, (0, 2), (1, 3))
    return t.reshape(t.shape[0], -1, *t.shape[3:])


def sink_softmax(logit_parts, sink):
    m = sink
    for l in logit_parts:
        m = jnp.maximum(m, jnp.max(l, axis=-1, keepdims=True))
    probs = [jnp.exp(l - m) for l in logit_parts]
    den = jnp.exp(sink - m)
    for p in probs:
        den = den + jnp.sum(p, axis=-1, keepdims=True)
    return [p / den for p in probs]


def _block_diag(x, w, b):
    lead = x.shape[:-1]
    xb = x.reshape(*lead, LRU_BLOCKS, LRU_BLOCK)
    return jnp.einsum('...nd,nde->...ne', xb, w).reshape(*lead, LRU_WIDTH) + b


def _linear_scan(a, b, h0):
    def combine(left, right):
        a_l, b_l = left
        a_r, b_r = right
        return a_r * a_l, a_r * b_l + b_r
    a_cum, h = lax.associative_scan(combine, (a, b), axis=1)
    return h + a_cum * h0[:, None, :]


def _rglru_direction(xs, w_a, b_a, w_x, b_x, lam, h0):
    xs = xs.astype(F32)
    r = jax.nn.sigmoid(_block_diag(xs, w_a.astype(F32), b_a.astype(F32)))
    i = jax.nn.sigmoid(_block_diag(xs, w_x.astype(F32), b_x.astype(F32)))
    log_a = -LRU_C * r * jax.nn.softplus(-lam.astype(F32))
    a = jnp.exp(log_a)
    b = jnp.sqrt(-jnp.expm1(2.0 * log_a)) * (i * xs)
    return _linear_scan(a, b, h0)


def rglru_mixer(h_c, h_l, w_in, conv_w, conv_b, gate_a_w, gate_a_b, gate_x_w, gate_x_b, lam, w_out, ctx_out):
    def branches(h):
        gate, xr = jnp.split(h @ w_in, 2, axis=-1)
        return jax.nn.gelu(gate), centred_dwconv(xr, conv_w) + conv_b
    gate_c, x_c = branches(h_c)
    gate_l, x_l = branches(h_l)
    h0 = jnp.zeros((h_l.shape[0], LRU_WIDTH), F32)
    ys_c, ys_l = [], []
    for d, f in enumerate((identity, flip_seq)):
        params = (gate_a_w[d], gate_a_b[d], gate_x_w[d], gate_x_b[d], lam[d])
        hc = _rglru_direction(f(x_c), *params, h0)
        hl = _rglru_direction(f(x_l), *params, hc[:, -1])
        ys_c.append(f(hc))
        ys_l.append(f(hl))
    y_l = (gate_l.astype(F32) * (ys_l[0] + ys_l[1])).astype(h_l.dtype) @ w_out
    y_c = (gate_c.astype(F32) * (ys_c[0] + ys_c[1])).astype(h_c.dtype) @ w_out if ctx_out else None
    return y_c, y_l


def _gated_delta_chunked(q, k, v, g, beta, S0):
    C = DN_CHUNK
    DV = v.shape[-1]
    qc, kc, vc = to_chunks(q, C), to_chunks(k, C), to_chunks(v, C)
    gcum = jnp.cumsum(to_chunks(g, C), axis=-1)
    bc = to_chunks(beta, C)
    tri = jnp.tril(jnp.ones((C, C), bool))
    strict = jnp.tril(jnp.ones((C, C), bool), -1)
    diff = gcum[..., :, None] - gcum[..., None, :]
    decay = jnp.where(tri, jnp.exp(jnp.where(tri, diff, 0.0)), 0.0)
    kb = kc * bc[..., None]
    m = jnp.where(strict, jnp.einsum('nbhck,nbhsk->nbhcs', kb, kc) * decay, 0.0)
    eye = jnp.eye(C, dtype=F32)
    rhs = jnp.concatenate([vc * bc[..., None], kb * jnp.exp(gcum)[..., None]], axis=-1)
    sol = lax.linalg.triangular_solve(m + eye, rhs, left_side=True, lower=True, unit_diagonal=True)
    u, w = sol[..., :DV], sol[..., DV:]
    a_intra = jnp.where(tri, jnp.einsum('nbhck,nbhsk->nbhcs', qc, kc) * decay, 0.0)

    def step(S, inp):
        qi, ki, ui, wi, gi, ai = inp
        v_new = ui - jnp.einsum('bhck,bhkv->bhcv', wi, S)
        o = jnp.einsum('bhck,bhkv->bhcv', qi * jnp.exp(gi)[..., None], S) + jnp.einsum('bhcs,bhsv->bhcv', ai, v_new)
        g_last = gi[..., -1:]
        S = S * jnp.exp(g_last)[..., None] + jnp.einsum('bhck,bhcv->bhkv', ki * jnp.exp(g_last - gi)[..., None], v_new)
        return S, o

    S_fin, o = lax.scan(step, S0, (qc, kc, u, w, gcum, a_intra))
    return from_chunks(o), S_fin


def gated_deltanet_mixer(h_c, h_l, w_in, conv_w, a_log, dt_bias, out_norm, w_out, ctx_out):
    H = DN_HEADS
    HK, HV = DN_HEADS * DN_DK, DN_HEADS * DN_DV

    def project(h):
        Bb, L = h.shape[:2]
        qkv, z, beta_raw, a_raw = jnp.split(h @ w_in, [2 * HK + HV, 2 * HK + 2 * HV, 2 * HK + 2 * HV + 2 * H], axis=-1)
        qkv = jax.nn.silu(centred_dwconv(qkv, conv_w))
        q, k, v = jnp.split(qkv, [HK, 2 * HK], axis=-1)
        q = l2_normalize(q.reshape(Bb, L, H, DN_DK).astype(F32)) * DN_DK ** -0.5
        k = l2_normalize(k.reshape(Bb, L, H, DN_DK).astype(F32))
        v = v.reshape(Bb, L, H, DN_DV).astype(F32)
        beta = jax.nn.sigmoid(beta_raw.astype(F32)).reshape(Bb, L, 2, H)
        g = -jnp.exp(a_log.astype(F32)) * jax.nn.softplus(a_raw.astype(F32).reshape(Bb, L, 2, H) + dt_bias.astype(F32))
        return q, k, v, z, beta, g

    qc, kc, vc, zc, bc, gc = project(h_c)
    ql, kl, vl, zl, bl, gl = project(h_l)
    S0 = jnp.zeros((h_l.shape[0], H, DN_DK, DN_DV), F32)
    o_c, o_l = [], []
    for d, f in enumerate((identity, flip_seq)):
        oc, Sc = _gated_delta_chunked(f(qc), f(kc), f(vc), f(gc[:, :, d]), f(bc[:, :, d]), S0)
        ol, _ = _gated_delta_chunked(f(ql), f(kl), f(vl), f(gl[:, :, d]), f(bl[:, :, d]), Sc)
        o_c.append(f(oc))
        o_l.append(f(ol))

    def finish(o, z):
        Bb, L = o.shape[:2]
        y = rms_norm(o, out_norm) * jax.nn.silu(z.astype(F32).reshape(Bb, L, H, DN_DV))
        return y.reshape(Bb, L, HV).astype(z.dtype) @ w_out

    y_l = finish(o_l[0] + o_l[1], zl)
    y_c = finish(o_c[0] + o_c[1], zc) if ctx_out else None
    return y_c, y_l


def _retention_chunked(q, k, v, log_gamma, S0, include_diag):
    C = RET_CHUNK
    qc, kc, vc = to_chunks(q, C), to_chunks(k, C), to_chunks(v, C)
    idx = jnp.arange(C)
    diff = idx[:, None] - idx[None, :]
    mask = diff >= 0 if include_diag else diff > 0
    dmat = jnp.where(mask, jnp.exp(jnp.maximum(diff, 0).astype(F32) * log_gamma[:, None, None]), 0.0)
    to_state = jnp.exp((C - 1 - idx).astype(F32) * log_gamma[:, None])
    from_state = jnp.exp((idx + 1).astype(F32) * log_gamma[:, None])
    chunk_decay = jnp.exp(C * log_gamma)[:, None, None]
    intra = jnp.einsum('nbhcs,nbhsv->nbhcv', jnp.einsum('nbhck,nbhsk->nbhcs', qc, kc) * dmat, vc)

    def step(S, inp):
        qi, ki, vi = inp
        cross = jnp.einsum('bhck,bhkv->bhcv', qi, S) * from_state[..., None]
        S = S * chunk_decay + jnp.einsum('bhck,bhcv->bhkv', ki * to_state[..., None], vi)
        return S, cross

    S_fin, cross = lax.scan(step, S0, (qc, kc, vc))
    return from_chunks(intra + cross), S_fin


def retention_mixer(h_c, h_l, w_in, out_norm, w_out, rope, ctx_out):
    H = RET_HEADS
    HK, HV = RET_HEADS * RET_DK, RET_HEADS * RET_DV
    log_gamma = jnp.log1p(-jnp.exp2(-5.0 - jnp.arange(H, dtype=F32)))

    def project(h, rope_tab):
        Bb, L = h.shape[:2]
        q, k, v, gate = jnp.split(h @ w_in, [HK, 2 * HK, 2 * HK + HV], axis=-1)
        q = q.reshape(Bb, L, H, RET_DK)
        k = k.reshape(Bb, L, H, RET_DK)
        if rope_tab is not None:
            q, k = apply_axial_rope(q, rope_tab), apply_axial_rope(k, rope_tab)
        return q.astype(F32), k.astype(F32) * RET_DK ** -0.5, v.reshape(Bb, L, H, RET_DV).astype(F32), gate

    qc, kc, vc, gate_c = project(h_c, None)
    ql, kl, vl, gate_l = project(h_l, rope)
    S0 = jnp.zeros((h_l.shape[0], H, RET_DK, RET_DV), F32)
    o_c, o_l = [], []
    for d, f in enumerate((identity, flip_seq)):
        oc, Sc = _retention_chunked(f(qc), f(kc), f(vc), log_gamma, S0, d == 0)
        ol, _ = _retention_chunked(f(ql), f(kl), f(vl), log_gamma, Sc, d == 0)
        o_c.append(f(oc))
        o_l.append(f(ol))

    def finish(o, gate):
        Bb, L = o.shape[:2]
        y = rms_norm(o, out_norm.reshape(H, RET_DV)).reshape(Bb, L, HV)
        return (y * jax.nn.silu(gate.astype(F32))).astype(gate.dtype) @ w_out

    y_l = finish(o_l[0] + o_l[1], gate_l)
    y_c = finish(o_c[0] + o_c[1], gate_c) if ctx_out else None
    return y_c, y_l


def window_gqa_mixer(h_c, h_l, w_in, q_norm, k_norm, sink, w_out, rope, ctx_out):
    QW, KW = ATT_HEADS * ATT_HD, ATT_KV_HEADS * ATT_HD
    G = ATT_HEADS // ATT_KV_HEADS
    scale = ATT_HD ** -0.5

    def project(h):
        Bb, L = h.shape[:2]
        q, k, v = jnp.split(h @ w_in, [QW, QW + KW], axis=-1)
        q = rms_norm(q.reshape(Bb, L, ATT_HEADS, ATT_HD), q_norm)
        k = rms_norm(k.reshape(Bb, L, ATT_KV_HEADS, ATT_HD), k_norm)
        return q, k, v.reshape(Bb, L, ATT_KV_HEADS, ATT_HD)

    qc, kc, vc = project(h_c)
    ql, kl, vl = project(h_l)
    ql, kl = apply_axial_rope(ql, rope), apply_axial_rope(kl, rope)
    kc32, vc32 = kc.astype(F32), vc.astype(F32)
    sink_g = sink.astype(F32).reshape(ATT_KV_HEADS, G, 1, 1)

    Bb, S = h_l.shape[:2]
    Bk = ATT_BLOCK
    nb = S // Bk
    qb = ql.astype(F32).reshape(Bb, nb, Bk, ATT_KV_HEADS, G, ATT_HD) * scale

    def bands(t):
        tp = jnp.pad(t.astype(F32), ((0, 0), (Bk, Bk), (0, 0), (0, 0))).reshape(Bb, nb + 2, Bk, ATT_KV_HEADS, ATT_HD)
        return jnp.concatenate([tp[:, :-2], tp[:, 1:-1], tp[:, 2:]], axis=2)

    kb, vb = bands(kl), bands(vl)
    blk = jnp.arange(nb)[:, None, None] * Bk
    qpos = blk + jnp.arange(Bk)[None, :, None]
    kpos = blk - Bk + jnp.arange(3 * Bk)[None, None, :]
    band = (jnp.abs(qpos - kpos) <= WINDOW) & (kpos >= 0) & (kpos < S)
    s_loc = jnp.einsum('bnqkgd,bnskd->bnkgqs', qb, kb)
    s_loc = jnp.where(band[None, :, None, None], s_loc, -jnp.inf)
    s_ctx = jnp.einsum('bnqkgd,bckd->bnkgqc', qb, kc32)
    p_loc, p_ctx = sink_softmax((s_loc, s_ctx), sink_g)
    o = jnp.einsum('bnkgqs,bnskd->bnqkgd', p_loc, vb) + jnp.einsum('bnkgqc,bckd->bnqkgd', p_ctx, vc32)
    y_l = o.reshape(Bb, S, QW).astype(h_l.dtype) @ w_out

    y_c = None
    if ctx_out:
        n_ctx = h_c.shape[1]
        qcg = qc.astype(F32).reshape(Bb, n_ctx, ATT_KV_HEADS, G, ATT_HD) * scale
        (p,) = sink_softmax((jnp.einsum('bqkgd,bskd->bkgqs', qcg, kc32),), sink_g)
        y_c = jnp.einsum('bkgqs,bskd->bqkgd', p, vc32).reshape(Bb, n_ctx, QW).astype(h_c.dtype) @ w_out
    return y_c, y_l


def route(h, router_w, router_b):
    n = h.shape[0]
    scores = jax.nn.sigmoid(h.astype(F32) @ router_w.astype(F32))
    biased = (scores + router_b.astype(F32)).reshape(n, N_GROUPS, EXPERTS_PER_GROUP)
    group_score = jnp.sum(lax.top_k(biased, TOP_K)[0], axis=-1)
    group = jnp.argmax(group_score, axis=-1)
    in_group = jnp.take_along_axis(biased, group[:, None, None], axis=1)[:, 0]
    expert_ids = group[:, None] * EXPERTS_PER_GROUP + lax.top_k(in_group, TOP_K)[1]
    gate = jnp.take_along_axis(scores, expert_ids, axis=1)
    return expert_ids, gate / jnp.sum(gate, axis=-1, keepdims=True)


def moe_ffn(h, router_w, router_b, w_gate, w_up, w_down):
    N, D = h.shape
    expert_ids, weights = route(h, router_w, router_b)
    NK = N * TOP_K
    flat_e = expert_ids.reshape(NK)
    order = jnp.argsort(flat_e)
    e_sorted = flat_e[order]
    tok_sorted = order // TOP_K
    counts = jnp.bincount(flat_e, length=N_EXPERTS)
    padded = (counts + MOE_BLOCK - 1) // MOE_BLOCK * MOE_BLOCK
    pad_end = jnp.cumsum(padded)
    pad_start = pad_end - padded
    start = jnp.cumsum(counts) - counts
    dest = pad_start[e_sorted] + jnp.arange(NK) - start[e_sorted]
    n_blocks = (NK + N_EXPERTS * (MOE_BLOCK - 1)) // MOE_BLOCK
    n_slots = n_blocks * MOE_BLOCK
    slot_tok = jnp.zeros((n_slots,), jnp.int32).at[dest].set(tok_sorted)
    block_expert = jnp.minimum(jnp.searchsorted(pad_end, jnp.arange(n_blocks) * MOE_BLOCK, side='right'), N_EXPERTS - 1)
    xs = h[slot_tok].reshape(n_blocks, MOE_BLOCK, D)

    def expert_block(args):
        xb, e = args
        return (jax.nn.silu(xb @ w_gate[e]) * (xb @ w_up[e])) @ w_down[e]

    ys = lax.map(expert_block, (xs, block_expert)).reshape(n_slots, D)
    contrib = ys[dest] * weights.reshape(NK)[order][:, None].astype(h.dtype)
    return jnp.zeros_like(h).at[tok_sorted].add(contrib)


def setup_inputs(seed: int = 0) -> dict:
    key = jax.random.key(seed)
    keys = iter(jax.random.split(key, 64))

    def normal(shape, scale):
        return jax.random.normal(next(keys), shape, F32) * scale

    def uniform(shape, lo, hi):
        return jax.random.uniform(next(keys), shape, F32, lo, hi)

    D, LPM = D_MODEL, LAYERS_PER_MIXER
    lam_sig = uniform((LPM, 2, LRU_WIDTH), 0.9, 0.999) ** (1.0 / LRU_C)
    dt = jnp.exp(uniform((LPM, 2, DN_HEADS), math.log(1e-3), math.log(1e-1)))
    return {
        'x': normal((BATCH, SEQ, D), 1.0),
        'c': normal((BATCH, D), 1.0),
        'ctx': normal((BATCH, CTX_LEN, D), 1.0),
        'c_ctx': normal((D,), 1.0),
        'mod_w': normal((DEPTH, D, 6 * D), 0.5 * D ** -0.5),
        'mod_b': normal((DEPTH, 6 * D), 0.01),
        'norm_mix': 1.0 + normal((DEPTH, D), 0.02),
        'norm_ffn': 1.0 + normal((DEPTH, D), 0.02),
        'lru_w_in': normal((LPM, D, 2 * LRU_WIDTH), D ** -0.5),
        'lru_conv_w': normal((LPM, CONV_W, LRU_WIDTH), CONV_W ** -0.5),
        'lru_conv_b': normal((LPM, LRU_WIDTH), 0.01),
        'lru_gate_a_w': normal((LPM, 2, LRU_BLOCKS, LRU_BLOCK, LRU_BLOCK), LRU_BLOCK ** -0.5),
        'lru_gate_a_b': normal((LPM, 2, LRU_WIDTH), 0.01),
        'lru_gate_x_w': normal((LPM, 2, LRU_BLOCKS, LRU_BLOCK, LRU_BLOCK), LRU_BLOCK ** -0.5),
        'lru_gate_x_b': normal((LPM, 2, LRU_WIDTH), 0.01),
        'lru_lambda': jnp.log(lam_sig) - jnp.log1p(-lam_sig),
        'lru_w_out': normal((LPM, LRU_WIDTH, D), LRU_WIDTH ** -0.5),
        'dn_w_in': normal((LPM, D, DN_IN), D ** -0.5),
        'dn_conv_w': normal((LPM, CONV_W, 2 * DN_HEADS * DN_DK + DN_HEADS * DN_DV), CONV_W ** -0.5),
        'dn_a_log': jnp.log(uniform((LPM, 2, DN_HEADS), 1.0, 16.0)),
        'dn_dt_bias': dt + jnp.log(-jnp.expm1(-dt)),
        'dn_norm': 1.0 + normal((LPM, DN_DV), 0.02),
        'dn_w_out': normal((LPM, DN_HEADS * DN_DV, D), (DN_HEADS * DN_DV) ** -0.5),
        'ret_w_in': normal((LPM, D, RET_IN), D ** -0.5),
        'ret_norm': 1.0 + normal((LPM, RET_HEADS * RET_DV), 0.02),
        'ret_w_out': normal((LPM, RET_HEADS * RET_DV, D), (RET_HEADS * RET_DV) ** -0.5),
        'att_w_in': normal((LPM, D, ATT_IN), D ** -0.5),
        'att_q_norm': 1.0 + normal((LPM, ATT_HD), 0.02),
        'att_k_norm': 1.0 + normal((LPM, ATT_HD), 0.02),
        'att_sink': normal((LPM, ATT_HEADS), 0.5),
        'att_w_out': normal((LPM, ATT_HEADS * ATT_HD, D), (ATT_HEADS * ATT_HD) ** -0.5),
        'router_w': normal((D, N_EXPERTS), D ** -0.5),
        'router_b': normal((N_EXPERTS,), 0.01),
        'moe_w_gate': normal((DEPTH, N_EXPERTS, D, D_EXPERT), D ** -0.5),
        'moe_w_up': normal((DEPTH, N_EXPERTS, D, D_EXPERT), D ** -0.5),
        'moe_w_down': normal((DEPTH, N_EXPERTS, D_EXPERT, D), D_EXPERT ** -0.5),
    }


def reference(x, c, ctx, c_ctx, mod_w, mod_b, norm_mix, norm_ffn,
              lru_w_in, lru_conv_w, lru_conv_b, lru_gate_a_w, lru_gate_a_b, lru_gate_x_w, lru_gate_x_b,
              lru_lambda, lru_w_out,
              dn_w_in, dn_conv_w, dn_a_log, dn_dt_bias, dn_norm, dn_w_out,
              ret_w_in, ret_norm, ret_w_out,
              att_w_in, att_q_norm, att_k_norm, att_sink, att_w_out,
              router_w, router_b, moe_w_gate, moe_w_up, moe_w_down):
    Bb, S, D = x.shape
    n_ctx = ctx.shape[1]
    n_rows = S // GRID_W
    rows = jnp.repeat(jnp.arange(n_rows), GRID_W)
    cols = jnp.tile(jnp.arange(GRID_W), n_rows)
    rope_att = axial_rope_tables(rows, cols, ATT_HD)
    rope_ret = axial_rope_tables(rows, cols, RET_DK)
    silu_c = jax.nn.silu(c)
    silu_cc = jax.nn.silu(c_ctx)

    for i in range(DEPTH):
        ctx_out = i < DEPTH - 1
        kind, j = i % N_MIXERS, i // N_MIXERS
        sh1, sc1, g1, sh2, sc2, g2 = jnp.split((silu_c @ mod_w[i] + mod_b[i])[:, None, :], 6, axis=-1)
        csh1, csc1, cg1, csh2, csc2, cg2 = jnp.split(silu_cc @ mod_w[i] + mod_b[i], 6, axis=-1)
        h_l = rms_norm(x, norm_mix[i]) * (1 + sc1) + sh1
        h_c = rms_norm(ctx, norm_mix[i]) * (1 + csc1) + csh1

        if kind == 0:
            y_c, y_l = rglru_mixer(h_c, h_l, lru_w_in[j], lru_conv_w[j], lru_conv_b[j], lru_gate_a_w[j],
                                   lru_gate_a_b[j], lru_gate_x_w[j], lru_gate_x_b[j], lru_lambda[j],
                                   lru_w_out[j], ctx_out)
        elif kind == 1:
            y_c, y_l = gated_deltanet_mixer(h_c, h_l, dn_w_in[j], dn_conv_w[j], dn_a_log[j], dn_dt_bias[j],
                                            dn_norm[j], dn_w_out[j], ctx_out)
        elif kind == 2:
            y_c, y_l = retention_mixer(h_c, h_l, ret_w_in[j], ret_norm[j], ret_w_out[j], rope_ret, ctx_out)
        else:
            y_c, y_l = window_gqa_mixer(h_c, h_l, att_w_in[j], att_q_norm[j], att_k_norm[j], att_sink[j],
                                        att_w_out[j], rope_att, ctx_out)

        x = x + g1 * y_l
        h2_l = (rms_norm(x, norm_ffn[i]) * (1 + sc2) + sh2).reshape(Bb * S, D)
        if ctx_out:
            ctx = ctx + cg1 * y_c
            h2_c = (rms_norm(ctx, norm_ffn[i]) * (1 + csc2) + csh2).reshape(Bb * n_ctx, D)
            f = moe_ffn(jnp.concatenate([h2_c, h2_l], axis=0), router_w, router_b,
                        moe_w_gate[i], moe_w_up[i], moe_w_down[i])
            ctx = ctx + cg2 * f[:Bb * n_ctx].reshape(Bb, n_ctx, D)
            f_l = f[Bb * n_ctx:]
        else:
            f_l = moe_ffn(h2_l, router_w, router_b, moe_w_gate[i], moe_w_up[i], moe_w_down[i])
        x = x + g2 * f_l.reshape(Bb, S, D)
    return x
```

```python
import functools
import math

import jax
import jax.numpy as jnp
from jax import lax
from jax.experimental import pallas as pl
from jax.experimental.pallas import tpu as pltpu

D_MODEL = 2048
DEPTH = 4
GRID_W = 64
N_MIXERS = 4
NORM_EPS = 1e-6
ROPE_BASE = 10000.0
CONV_W = 4
CONV_PAD = (CONV_W // 2, CONV_W - 1 - CONV_W // 2)

LRU_WIDTH = D_MODEL
LRU_BLOCKS = 8
LRU_BLOCK = LRU_WIDTH // LRU_BLOCKS
LRU_C = 8.0

DN_HEADS = 16
DN_DK = D_MODEL // DN_HEADS
DN_DV = D_MODEL // DN_HEADS
DN_CHUNK = 64

RET_HEADS = 8
RET_DK = D_MODEL // RET_HEADS
RET_DV = 2 * D_MODEL // RET_HEADS
RET_CHUNK = 128

ATT_HEADS = 16
ATT_KV_HEADS = 4
ATT_HD = D_MODEL // ATT_HEADS
WINDOW = 128
ATT_BLOCK = 128

N_EXPERTS = 32
N_GROUPS = 8
EXPERTS_PER_GROUP = N_EXPERTS // N_GROUPS
TOP_K = 2
D_EXPERT = 512
MOE_BLOCK = 128

F32 = jnp.float32
BF16 = jnp.bfloat16

VMEM_LIMIT_BYTES = 56 * 1024 * 1024


def _mm_kernel(x_ref, w_ref, o_ref, wb_ref):
    @pl.when(pl.program_id(1) == 0)
    def _():
        wb_ref[...] = w_ref[...].astype(BF16)

    o_ref[...] = jnp.dot(x_ref[...], wb_ref[...], preferred_element_type=F32).astype(o_ref.dtype)


def _matmul(x, w, layer=None, tm=512, tn=512):
    M, K = x.shape
    N = w.shape[-1]
    tm = min(tm, M)
    tn = min(tn, N)
    assert M % tm == 0 and N % tn == 0, (M, N, tm, tn)
    if layer is None:
        w_spec = pl.BlockSpec((K, tn), lambda n, m: (0, n))
    else:
        w_spec = pl.BlockSpec((None, K, tn), lambda n, m: (layer, 0, n))
    return pl.pallas_call(
        _mm_kernel,
        grid=(N // tn, M // tm),
        in_specs=[pl.BlockSpec((tm, K), lambda n, m: (m, 0)), w_spec],
        out_specs=pl.BlockSpec((tm, tn), lambda n, m: (m, n)),
        out_shape=jax.ShapeDtypeStruct((M, N), F32),
        scratch_shapes=[pltpu.VMEM((K, tn), BF16)],
        compiler_params=pltpu.CompilerParams(
            dimension_semantics=("arbitrary", "arbitrary"),
            vmem_limit_bytes=VMEM_LIMIT_BYTES),
        name="dense_matmul",
    )(x, w)


def _proj(h_c, h_l, w):
    Bb, Lc, D = h_c.shape
    Ll = h_l.shape[1]
    rows = jnp.concatenate([h_c.reshape(Bb * Lc, D), h_l.reshape(Bb * Ll, D)], axis=0).astype(BF16)
    N = w.shape[1]
    n_main = N // 512 * 512
    out = _matmul(rows, w[:, :n_main] if n_main != N else w)
    if n_main != N:
        tail = jnp.pad(w[:, n_main:], ((0, 0), (0, 128 - (N - n_main))))
        out = jnp.concatenate([out, _matmul(rows, tail)[:, :N - n_main]], axis=1)
    return out[:Bb * Lc].reshape(Bb, Lc, N), out[Bb * Lc:].reshape(Bb, Ll, N)


def _out_proj(y, w):
    Bb, L, K = y.shape
    return _matmul(y.reshape(Bb * L, K).astype(BF16), w).reshape(Bb, L, w.shape[1])


def _moe_kernel(be_ref, xs_ref, wg_ref, wu_ref, wd_ref, o_ref, wgb, wub, wdb):
    b = pl.program_id(0)
    e = be_ref[b]
    prev = be_ref[jnp.maximum(b - 1, 0)]

    @pl.when((b == 0) | (e != prev))
    def _():
        wgb[...] = wg_ref[...].astype(BF16)
        wub[...] = wu_ref[...].astype(BF16)
        wdb[...] = wd_ref[...].astype(BF16)

    x = xs_ref[...]
    g = jnp.dot(x, wgb[...], preferred_element_type=F32)
    u = jnp.dot(x, wub[...], preferred_element_type=F32)
    a = (g * jax.nn.sigmoid(g) * u).astype(BF16)
    o_ref[...] = jnp.dot(a, wdb[...], preferred_element_type=F32)


def _moe_experts(xs, block_expert, w_gate, w_up, w_down, layer):
    n_slots, D = xs.shape
    n_blocks = n_slots // MOE_BLOCK
    DE = w_gate.shape[-1]
    grid_spec = pltpu.PrefetchScalarGridSpec(
        num_scalar_prefetch=1,
        grid=(n_blocks,),
        in_specs=[pl.BlockSpec((MOE_BLOCK, D), lambda b, be: (b, 0)),
                  pl.BlockSpec((None, None, D, DE), lambda b, be: (layer, be[b], 0, 0)),
                  pl.BlockSpec((None, None, D, DE), lambda b, be: (layer, be[b], 0, 0)),
                  pl.BlockSpec((None, None, DE, D), lambda b, be: (layer, be[b], 0, 0))],
        out_specs=pl.BlockSpec((MOE_BLOCK, D), lambda b, be: (b, 0)),
        scratch_shapes=[pltpu.VMEM((D, DE), BF16), pltpu.VMEM((D, DE), BF16), pltpu.VMEM((DE, D), BF16)],
    )
    return pl.pallas_call(
        _moe_kernel,
        grid_spec=grid_spec,
        out_shape=jax.ShapeDtypeStruct((n_slots, D), F32),
        compiler_params=pltpu.CompilerParams(
            dimension_semantics=("arbitrary",),
            vmem_limit_bytes=VMEM_LIMIT_BYTES),
        name="moe_experts",
    )(block_expert.astype(jnp.int32), xs, w_gate, w_up, w_down)


def rms_norm(x, g):
    xf = x.astype(F32)
    y = xf * lax.rsqrt(jnp.mean(jnp.square(xf), axis=-1, keepdims=True) + NORM_EPS)
    return (y * g.astype(F32)).astype(x.dtype)


def l2_normalize(x):
    return x * lax.rsqrt(jnp.sum(jnp.square(x), axis=-1, keepdims=True) + NORM_EPS)


def identity(t):
    return t


def flip_seq(t):
    return jnp.flip(t, axis=1)


def centred_dwconv(x, w):
    L = x.shape[1]
    xp = jnp.pad(x, ((0, 0), CONV_PAD, (0, 0)))
    y = xp[:, 0:L] * w[0]
    for tap in range(1, CONV_W):
        y = y + xp[:, tap:tap + L] * w[tap]
    return y


def axial_rope_tables(rows, cols, head_dim):
    quarter = head_dim // 4
    inv_freq = ROPE_BASE ** (-jnp.arange(quarter, dtype=F32) / quarter)
    ang_r = rows.astype(F32)[:, None] * inv_freq
    ang_c = cols.astype(F32)[:, None] * inv_freq
    return (jnp.cos(ang_r), jnp.sin(ang_r), jnp.cos(ang_c), jnp.sin(ang_c))


def apply_axial_rope(x, tables):
    cos_r, sin_r, cos_c, sin_c = (t[:, None, :] for t in tables)
    hd = x.shape[-1]
    q4 = hd // 4
    xf = x.astype(F32)

    def rot(v, cos, sin):
        v1, v2 = v[..., :q4], v[..., q4:]
        return jnp.concatenate([v1 * cos - v2 * sin, v2 * cos + v1 * sin], axis=-1)

    out = jnp.concatenate([rot(xf[..., :hd // 2], cos_r, sin_r), rot(xf[..., hd // 2:], cos_c, sin_c)], axis=-1)
    return out.astype(x.dtype)


def to_chunks(t, size):
    Bb, L = t.shape[:2]
    t = t.reshape(Bb, L // size, size, *t.shape[2:])
    return jnp.moveaxis(t, (1, 3), (0, 2))


def from_chunks(t):
    t = jnp.moveaxis(t, (0, 2), (1, 3))
    return t.reshape(t.shape[0], -1, *t.shape[3:])


def sink_softmax(logit_parts, sink):
    m = sink
    for l in logit_parts:
        m = jnp.maximum(m, jnp.max(l, axis=-1, keepdims=True))
    probs = [jnp.exp(l - m) for l in logit_parts]
    den = jnp.exp(sink - m)
    for p in probs:
        den = den + jnp.sum(p, axis=-1, keepdims=True)
    return [p / den for p in probs]


def _block_diag(x, w, b):
    lead = x.shape[:-1]
    xb = x.reshape(*lead, LRU_BLOCKS, LRU_BLOCK)
    return jnp.einsum('...nd,nde->...ne', xb, w).reshape(*lead, LRU_WIDTH) + b


def _linear_scan(a, b, h0):
    def combine(left, right):
        a_l, b_l = left
        a_r, b_r = right
        return a_r * a_l, a_r * b_l + b_r
    a_cum, h = lax.associative_scan(combine, (a, b), axis=1)
    return h + a_cum * h0[:, None, :]


def _rglru_direction(xs, w_a, b_a, w_x, b_x, lam, h0):
    xs = xs.astype(F32)
    r = jax.nn.sigmoid(_block_diag(xs, w_a.astype(F32), b_a.astype(F32)))
    i = jax.nn.sigmoid(_block_diag(xs, w_x.astype(F32), b_x.astype(F32)))
    log_a = -LRU_C * r * jax.nn.softplus(-lam.astype(F32))
    a = jnp.exp(log_a)
    b = jnp.sqrt(-jnp.expm1(2.0 * log_a)) * (i * xs)
    return _linear_scan(a, b, h0)


def rglru_mixer(h_c, h_l, w_in, conv_w, conv_b, gate_a_w, gate_a_b, gate_x_w, gate_x_b, lam, w_out, ctx_out):
    p_c, p_l = _proj(h_c, h_l, w_in)

    def branches(p):
        gate, xr = jnp.split(p, 2, axis=-1)
        return jax.nn.gelu(gate), centred_dwconv(xr, conv_w) + conv_b
    gate_c, x_c = branches(p_c)
    gate_l, x_l = branches(p_l)
    h0 = jnp.zeros((h_l.shape[0], LRU_WIDTH), F32)
    ys_c, ys_l = [], []
    for d, f in enumerate((identity, flip_seq)):
        params = (gate_a_w[d], gate_a_b[d], gate_x_w[d], gate_x_b[d], lam[d])
        hc = _rglru_direction(f(x_c), *params, h0)
        hl = _rglru_direction(f(x_l), *params, hc[:, -1])
        ys_c.append(f(hc))
        ys_l.append(f(hl))
    y_l = _out_proj(gate_l.astype(F32) * (ys_l[0] + ys_l[1]), w_out)
    y_c = _out_proj(gate_c.astype(F32) * (ys_c[0] + ys_c[1]), w_out) if ctx_out else None
    return y_c, y_l


def _gated_delta_chunked(q, k, v, g, beta, S0):
    C = DN_CHUNK
    DV = v.shape[-1]
    qc, kc, vc = to_chunks(q, C), to_chunks(k, C), to_chunks(v, C)
    gcum = jnp.cumsum(to_chunks(g, C), axis=-1)
    bc = to_chunks(beta, C)
    tri = jnp.tril(jnp.ones((C, C), bool))
    strict = jnp.tril(jnp.ones((C, C), bool), -1)
    diff = gcum[..., :, None] - gcum[..., None, :]
    decay = jnp.where(tri, jnp.exp(jnp.where(tri, diff, 0.0)), 0.0)
    kb = kc * bc[..., None]
    m = jnp.where(strict, jnp.einsum('nbhck,nbhsk->nbhcs', kb, kc) * decay, 0.0)
    eye = jnp.eye(C, dtype=F32)
    rhs = jnp.concatenate([vc * bc[..., None], kb * jnp.exp(gcum)[..., None]], axis=-1)
    sol = lax.linalg.triangular_solve(m + eye, rhs, left_side=True, lower=True, unit_diagonal=True)
    u, w = sol[..., :DV], sol[..., DV:]
    a_intra = jnp.where(tri, jnp.einsum('nbhck,nbhsk->nbhcs', qc, kc) * decay, 0.0)

    def step(S, inp):
        qi, ki, ui, wi, gi, ai = inp
        v_new = ui - jnp.einsum('bhck,bhkv->bhcv', wi, S)
        o = jnp.einsum('bhck,bhkv->bhcv', qi * jnp.exp(gi)[..., None], S) + jnp.einsum('bhcs,bhsv->bhcv', ai, v_new)
        g_last = gi[..., -1:]
        S = S * jnp.exp(g_last)[..., None] + jnp.einsum('bhck,bhcv->bhkv', ki * jnp.exp(g_last - gi)[..., None], v_new)
        return S, o

    S_fin, o = lax.scan(step, S0, (qc, kc, u, w, gcum, a_intra))
    return from_chunks(o), S_fin


def gated_deltanet_mixer(h_c, h_l, w_in, conv_w, a_log, dt_bias, out_norm, w_out, ctx_out):
    H = DN_HEADS
    HK, HV = DN_HEADS * DN_DK, DN_HEADS * DN_DV
    p_c, p_l = _proj(h_c, h_l, w_in)

    def project(p):
        Bb, L = p.shape[:2]
        qkv, z, beta_raw, a_raw = jnp.split(p, [2 * HK + HV, 2 * HK + 2 * HV, 2 * HK + 2 * HV + 2 * H], axis=-1)
        qkv = jax.nn.silu(centred_dwconv(qkv, conv_w))
        q, k, v = jnp.split(qkv, [HK, 2 * HK], axis=-1)
        q = l2_normalize(q.reshape(Bb, L, H, DN_DK).astype(F32)) * DN_DK ** -0.5
        k = l2_normalize(k.reshape(Bb, L, H, DN_DK).astype(F32))
        v = v.reshape(Bb, L, H, DN_DV).astype(F32)
        beta = jax.nn.sigmoid(beta_raw.astype(F32)).reshape(Bb, L, 2, H)
        g = -jnp.exp(a_log.astype(F32)) * jax.nn.softplus(a_raw.astype(F32).reshape(Bb, L, 2, H) + dt_bias.astype(F32))
        return q, k, v, z, beta, g

    qc, kc, vc, zc, bc, gc = project(p_c)
    ql, kl, vl, zl, bl, gl = project(p_l)
    S0 = jnp.zeros((h_l.shape[0], H, DN_DK, DN_DV), F32)
    o_c, o_l = [], []
    for d, f in enumerate((identity, flip_seq)):
        oc, Sc = _gated_delta_chunked(f(qc), f(kc), f(vc), f(gc[:, :, d]), f(bc[:, :, d]), S0)
        ol, _ = _gated_delta_chunked(f(ql), f(kl), f(vl), f(gl[:, :, d]), f(bl[:, :, d]), Sc)
        o_c.append(f(oc))
        o_l.append(f(ol))

    def finish(o, z):
        Bb, L = o.shape[:2]
        y = rms_norm(o, out_norm) * jax.nn.silu(z.astype(F32).reshape(Bb, L, H, DN_DV))
        return _out_proj(y.reshape(Bb, L, HV), w_out)

    y_l = finish(o_l[0] + o_l[1], zl)
    y_c = finish(o_c[0] + o_c[1], zc) if ctx_out else None
    return y_c, y_l


def _retention_chunked(q, k, v, log_gamma, S0, include_diag):
    C = RET_CHUNK
    qc, kc, vc = to_chunks(q, C), to_chunks(k, C), to_chunks(v, C)
    idx = jnp.arange(C)
    diff = idx[:, None] - idx[None, :]
    mask = diff >= 0 if include_diag else diff > 0
    dmat = jnp.where(mask, jnp.exp(jnp.maximum(diff, 0).astype(F32) * log_gamma[:, None, None]), 0.0)
    to_state = jnp.exp((C - 1 - idx).astype(F32) * log_gamma[:, None])
    from_state = jnp.exp((idx + 1).astype(F32) * log_gamma[:, None])
    chunk_decay = jnp.exp(C * log_gamma)[:, None, None]
    intra = jnp.einsum('nbhcs,nbhsv->nbhcv', jnp.einsum('nbhck,nbhsk->nbhcs', qc, kc) * dmat, vc)

    def step(S, inp):
        qi, ki, vi = inp
        cross = jnp.einsum('bhck,bhkv->bhcv', qi, S) * from_state[..., None]
        S = S * chunk_decay + jnp.einsum('bhck,bhcv->bhkv', ki * to_state[..., None], vi)
        return S, cross

    S_fin, cross = lax.scan(step, S0, (qc, kc, vc))
    return from_chunks(intra + cross), S_fin


def retention_mixer(h_c, h_l, w_in, out_norm, w_out, rope, ctx_out):
    H = RET_HEADS
    HK, HV = RET_HEADS * RET_DK, RET_HEADS * RET_DV
    log_gamma = jnp.log1p(-jnp.exp2(-5.0 - jnp.arange(H, dtype=F32)))
    p_c, p_l = _proj(h_c, h_l, w_in)

    def project(p, rope_tab):
        Bb, L = p.shape[:2]
        q, k, v, gate = jnp.split(p, [HK, 2 * HK, 2 * HK + HV], axis=-1)
        q = q.reshape(Bb, L, H, RET_DK)
        k = k.reshape(Bb, L, H, RET_DK)
        if rope_tab is not None:
            q, k = apply_axial_rope(q, rope_tab), apply_axial_rope(k, rope_tab)
        return q.astype(F32), k.astype(F32) * RET_DK ** -0.5, v.reshape(Bb, L, H, RET_DV).astype(F32), gate

    qc, kc, vc, gate_c = project(p_c, None)
    ql, kl, vl, gate_l = project(p_l, rope)
    S0 = jnp.zeros((h_l.shape[0], H, RET_DK, RET_DV), F32)
    o_c, o_l = [], []
    for d, f in enumerate((identity, flip_seq)):
        oc, Sc = _retention_chunked(f(qc), f(kc), f(vc), log_gamma, S0, d == 0)
        ol, _ = _retention_chunked(f(ql), f(kl), f(vl), log_gamma, Sc, d == 0)
        o_c.append(f(oc))
        o_l.append(f(ol))

    def finish(o, gate):
        Bb, L = o.shape[:2]
        y = rms_norm(o, out_norm.reshape(H, RET_DV)).reshape(Bb, L, HV)
        return _out_proj(y * jax.nn.silu(gate.astype(F32)), w_out)

    y_l = finish(o_l[0] + o_l[1], gate_l)
    y_c = finish(o_c[0] + o_c[1], gate_c) if ctx_out else None
    return y_c, y_l


def window_gqa_mixer(h_c, h_l, w_in, q_norm, k_norm, sink, w_out, rope, ctx_out):
    QW, KW = ATT_HEADS * ATT_HD, ATT_KV_HEADS * ATT_HD
    G = ATT_HEADS // ATT_KV_HEADS
    scale = ATT_HD ** -0.5
    p_c, p_l = _proj(h_c, h_l, w_in)

    def project(p):
        Bb, L = p.shape[:2]
        q, k, v = jnp.split(p, [QW, QW + KW], axis=-1)
        q = rms_norm(q.reshape(Bb, L, ATT_HEADS, ATT_HD), q_norm)
        k = rms_norm(k.reshape(Bb, L, ATT_KV_HEADS, ATT_HD), k_norm)
        return q, k, v.reshape(Bb, L, ATT_KV_HEADS, ATT_HD)

    qc, kc, vc = project(p_c)
    ql, kl, vl = project(p_l)
    ql, kl = apply_axial_rope(ql, rope), apply_axial_rope(kl, rope)
    kc32, vc32 = kc.astype(F32), vc.astype(F32)
    sink_g = sink.astype(F32).reshape(ATT_KV_HEADS, G, 1, 1)

    Bb, S = h_l.shape[:2]
    Bk = ATT_BLOCK
    nb = S // Bk
    qb = ql.astype(F32).reshape(Bb, nb, Bk, ATT_KV_HEADS, G, ATT_HD) * scale

    def bands(t):
        tp = jnp.pad(t.astype(F32), ((0, 0), (Bk, Bk), (0, 0), (0, 0))).reshape(Bb, nb + 2, Bk, ATT_KV_HEADS, ATT_HD)
        return jnp.concatenate([tp[:, :-2], tp[:, 1:-1], tp[:, 2:]], axis=2)

    kb, vb = bands(kl), bands(vl)
    blk = jnp.arange(nb)[:, None, None] * Bk
    qpos = blk + jnp.arange(Bk)[None, :, None]
    kpos = blk - Bk + jnp.arange(3 * Bk)[None, None, :]
    band = (jnp.abs(qpos - kpos) <= WINDOW) & (kpos >= 0) & (kpos < S)
    s_loc = jnp.einsum('bnqkgd,bnskd->bnkgqs', qb, kb)
    s_loc = jnp.where(band[None, :, None, None], s_loc, -jnp.inf)
    s_ctx = jnp.einsum('bnqkgd,bckd->bnkgqc', qb, kc32)
    p_loc, p_ctx = sink_softmax((s_loc, s_ctx), sink_g)
    o = jnp.einsum('bnkgqs,bnskd->bnqkgd', p_loc, vb) + jnp.einsum('bnkgqc,bckd->bnqkgd', p_ctx, vc32)
    y_l = _out_proj(o.reshape(Bb, S, QW), w_out)

    y_c = None
    if ctx_out:
        n_ctx = h_c.shape[1]
        qcg = qc.astype(F32).reshape(Bb, n_ctx, ATT_KV_HEADS, G, ATT_HD) * scale
        (p,) = sink_softmax((jnp.einsum('bqkgd,bskd->bkgqs', qcg, kc32),), sink_g)
        y_c = _out_proj(jnp.einsum('bkgqs,bskd->bqkgd', p, vc32).reshape(Bb, n_ctx, QW), w_out)
    return y_c, y_l


def route(h, router_w, router_b):
    n = h.shape[0]
    scores = jax.nn.sigmoid(jnp.dot(h.astype(F32), router_w.astype(F32), precision=lax.Precision.HIGHEST))
    biased = (scores + router_b.astype(F32)).reshape(n, N_GROUPS, EXPERTS_PER_GROUP)
    group_score = jnp.sum(lax.top_k(biased, TOP_K)[0], axis=-1)
    group = jnp.argmax(group_score, axis=-1)
    in_group = jnp.take_along_axis(biased, group[:, None, None], axis=1)[:, 0]
    expert_ids = group[:, None] * EXPERTS_PER_GROUP + lax.top_k(in_group, TOP_K)[1]
    gate = jnp.take_along_axis(scores, expert_ids, axis=1)
    return expert_ids, gate / jnp.sum(gate, axis=-1, keepdims=True)


def moe_ffn(h, router_w, router_b, w_gate, w_up, w_down, layer):
    N, D = h.shape
    expert_ids, weights = route(h, router_w, router_b)
    NK = N * TOP_K
    flat_e = expert_ids.reshape(NK)
    order = jnp.argsort(flat_e)
    e_sorted = flat_e[order]
    tok_sorted = order // TOP_K
    counts = jnp.bincount(flat_e, length=N_EXPERTS)
    padded = (counts + MOE_BLOCK - 1) // MOE_BLOCK * MOE_BLOCK
    pad_end = jnp.cumsum(padded)
    pad_start = pad_end - padded
    start = jnp.cumsum(counts) - counts
    dest = pad_start[e_sorted] + jnp.arange(NK) - start[e_sorted]
    n_blocks = (NK + N_EXPERTS * (MOE_BLOCK - 1)) // MOE_BLOCK
    n_slots = n_blocks * MOE_BLOCK
    slot_tok = jnp.zeros((n_slots,), jnp.int32).at[dest].set(tok_sorted)
    block_expert = jnp.minimum(jnp.searchsorted(pad_end, jnp.arange(n_blocks) * MOE_BLOCK, side='right'), N_EXPERTS - 1)
    xs = h.astype(BF16)[slot_tok]
    ys = _moe_experts(xs, block_expert, w_gate, w_up, w_down, layer)
    contrib = ys[dest] * weights.reshape(NK)[order][:, None].astype(h.dtype)
    return jnp.zeros_like(h).at[tok_sorted].add(contrib)


def kernel(x, c, ctx, c_ctx, mod_w, mod_b, norm_mix, norm_ffn, lru_w_in, lru_conv_w, lru_conv_b, lru_gate_a_w, lru_gate_a_b, lru_gate_x_w, lru_gate_x_b, lru_lambda, lru_w_out, dn_w_in, dn_conv_w, dn_a_log, dn_dt_bias, dn_norm, dn_w_out, ret_w_in, ret_norm, ret_w_out, att_w_in, att_q_norm, att_k_norm, att_sink, att_w_out, router_w, router_b, moe_w_gate, moe_w_up, moe_w_down):
    Bb, S, D = x.shape
    n_ctx = ctx.shape[1]
    n_rows = S // GRID_W
    rows = jnp.repeat(jnp.arange(n_rows), GRID_W)
    cols = jnp.tile(jnp.arange(GRID_W), n_rows)
    rope_att = axial_rope_tables(rows, cols, ATT_HD)
    rope_ret = axial_rope_tables(rows, cols, RET_DK)
    silu_c = jax.nn.silu(c)
    silu_cc = jax.nn.silu(c_ctx)
    cond = jnp.zeros((16, D), F32).at[:Bb].set(silu_c).at[Bb].set(silu_cc).astype(BF16)

    for i in range(DEPTH):
        ctx_out = i < DEPTH - 1
        kind, j = i % N_MIXERS, i // N_MIXERS
        mod = _matmul(cond, mod_w, layer=i) + mod_b[i]
        sh1, sc1, g1, sh2, sc2, g2 = jnp.split(mod[:Bb, None, :], 6, axis=-1)
        csh1, csc1, cg1, csh2, csc2, cg2 = jnp.split(mod[Bb], 6, axis=-1)
        h_l = rms_norm(x, norm_mix[i]) * (1 + sc1) + sh1
        h_c = rms_norm(ctx, norm_mix[i]) * (1 + csc1) + csh1

        if kind == 0:
            y_c, y_l = rglru_mixer(h_c, h_l, lru_w_in[j], lru_conv_w[j], lru_conv_b[j], lru_gate_a_w[j],
                                   lru_gate_a_b[j], lru_gate_x_w[j], lru_gate_x_b[j], lru_lambda[j],
                                   lru_w_out[j], ctx_out)
        elif kind == 1:
            y_c, y_l = gated_deltanet_mixer(h_c, h_l, dn_w_in[j], dn_conv_w[j], dn_a_log[j], dn_dt_bias[j],
                                            dn_norm[j], dn_w_out[j], ctx_out)
        elif kind == 2:
            y_c, y_l = retention_mixer(h_c, h_l, ret_w_in[j], ret_norm[j], ret_w_out[j], rope_ret, ctx_out)
        else:
            y_c, y_l = window_gqa_mixer(h_c, h_l, att_w_in[j], att_q_norm[j], att_k_norm[j], att_sink[j],
                                        att_w_out[j], rope_att, ctx_out)

        x = x + g1 * y_l
        h2_l = (rms_norm(x, norm_ffn[i]) * (1 + sc2) + sh2).reshape(Bb * S, D)
        if ctx_out:
            ctx = ctx + cg1 * y_c
            h2_c = (rms_norm(ctx, norm_ffn[i]) * (1 + csc2) + csh2).reshape(Bb * n_ctx, D)
            f = moe_ffn(jnp.concatenate([h2_c, h2_l], axis=0), router_w, router_b,
                        moe_w_gate, moe_w_up, moe_w_down, i)
            ctx = ctx + cg2 * f[:Bb * n_ctx].reshape(Bb, n_ctx, D)
            f_l = f[Bb * n_ctx:]
        else:
            f_l = moe_ffn(h2_l, router_w, router_b, moe_w_gate, moe_w_up, moe_w_down, i)
        x = x + g2 * f_l.reshape(Bb, S, D)
    return x
```

```python
import functools
import math

import jax
import jax.numpy as jnp
from jax import lax
from jax.experimental import pallas as pl
from jax.experimental.pallas import tpu as pltpu

D_MODEL = 2048
DEPTH = 4
GRID_W = 64
N_MIXERS = 4
NORM_EPS = 1e-6
ROPE_BASE = 10000.0
CONV_W = 4

LRU_WIDTH = D_MODEL
LRU_BLOCKS = 8
LRU_BLOCK = LRU_WIDTH // LRU_BLOCKS
LRU_C = 8.0

DN_HEADS = 16
DN_DK = D_MODEL // DN_HEADS
DN_DV = D_MODEL // DN_HEADS

RET_HEADS = 8
RET_DK = D_MODEL // RET_HEADS
RET_DV = 2 * D_MODEL // RET_HEADS

ATT_HEADS = 16
ATT_KV_HEADS = 4
ATT_HD = D_MODEL // ATT_HEADS
ATT_G = ATT_HEADS // ATT_KV_HEADS
WINDOW = 128

N_EXPERTS = 32
N_GROUPS = 8
EXPERTS_PER_GROUP = N_EXPERTS // N_GROUPS
TOP_K = 2
D_EXPERT = 512
MOE_BLOCK = 128

F32 = jnp.float32
BF16 = jnp.bfloat16

VMEM_LIMIT_BYTES = 56 * 1024 * 1024
SUBLANES = 8
LANES = 128
ROW_TILE = 256
HALO = SUBLANES


def _params(*sem):
    return pltpu.CompilerParams(dimension_semantics=sem, vmem_limit_bytes=VMEM_LIMIT_BYTES)


def _sigmoid(x):
    return 0.5 * jnp.tanh(0.5 * x) + 0.5


def _silu(x):
    return x * _sigmoid(x)


def _softplus(x):
    return jnp.maximum(x, 0.0) + jnp.log1p(jnp.exp(-jnp.abs(x)))


def _mm_kernel(x_ref, w_ref, o_ref, wb_ref):
    @pl.when(pl.program_id(1) == 0)
    def _():
        wb_ref[...] = w_ref[...].astype(BF16)

    o_ref[...] = jnp.dot(x_ref[...], wb_ref[...], preferred_element_type=F32)


def _matmul(x, w, layer=None, n_out=None, tm=512, tn=512):
    M, K = x.shape
    N = w.shape[-1] if n_out is None else n_out
    tm = min(tm, M)
    tn = min(tn, N)
    assert M % tm == 0 and N % tn == 0, (M, N, tm, tn)
    if layer is None:
        w_spec = pl.BlockSpec((K, tn), lambda n, m: (0, n))
    else:
        w_spec = pl.BlockSpec((None, K, tn), lambda n, m: (layer, 0, n))
    return pl.pallas_call(
        _mm_kernel,
        grid=(N // tn, M // tm),
        in_specs=[pl.BlockSpec((tm, K), lambda n, m: (m, 0)), w_spec],
        out_specs=pl.BlockSpec((tm, tn), lambda n, m: (m, n)),
        out_shape=jax.ShapeDtypeStruct((M, N), F32),
        scratch_shapes=[pltpu.VMEM((K, tn), BF16)],
        compiler_params=_params("arbitrary", "arbitrary"),
        name="dense_matmul",
    )(x, w)


def _norm_mod(x, normw, shift, scale):
    y = x * lax.rsqrt(jnp.mean(x * x, axis=-1, keepdims=True) + NORM_EPS)
    return (y * normw) * (1.0 + scale) + shift


def _prenorm_kernel(x_ref, nw_ref, mod_ref, h_ref):
    h_ref[...] = _norm_mod(x_ref[...], nw_ref[...], mod_ref[0:1, :], mod_ref[1:2, :]).astype(BF16)


def _prenorm(x, normw, modtile):
    M, D = x.shape
    return pl.pallas_call(
        _prenorm_kernel,
        grid=(M // ROW_TILE,),
        in_specs=[pl.BlockSpec((ROW_TILE, D), lambda t: (t, 0)),
                  pl.BlockSpec((1, D), lambda t: (0, 0)),
                  pl.BlockSpec((None, 8, D), lambda t: (t, 0, 0))],
        out_specs=pl.BlockSpec((ROW_TILE, D), lambda t: (t, 0)),
        out_shape=jax.ShapeDtypeStruct((M, D), BF16),
        compiler_params=_params("arbitrary"),
        name="prenorm",
    )(x, normw, modtile)


def _ffn_residual(x_ref, y_ref, w_ref, mod_ref):
    D = x_ref.shape[1]
    f = y_ref[:, 0:D] * w_ref[:, 0:1]
    for k in range(1, TOP_K):
        f = f + y_ref[:, k * D:(k + 1) * D] * w_ref[:, k:k + 1]
    return x_ref[...] + mod_ref[5:6, :] * f


def _resid_norm_kernel(x_ref, y_ref, w_ref, nw_ref, mod_ref, modn_ref, xo_ref, h_ref):
    x = _ffn_residual(x_ref, y_ref, w_ref, mod_ref)
    xo_ref[...] = x
    h_ref[...] = _norm_mod(x, nw_ref[...], modn_ref[0:1, :], modn_ref[1:2, :]).astype(BF16)


def _resid_norm(x, y, w, normw_next, modtile, modtile_next):
    M, D = x.shape
    row = pl.BlockSpec((ROW_TILE, D), lambda t: (t, 0))
    mod = pl.BlockSpec((None, 8, D), lambda t: (t, 0, 0))
    return pl.pallas_call(
        _resid_norm_kernel,
        grid=(M // ROW_TILE,),
        in_specs=[row, pl.BlockSpec((ROW_TILE, TOP_K * D), lambda t: (t, 0)),
                  pl.BlockSpec((ROW_TILE, LANES), lambda t: (t, 0)),
                  pl.BlockSpec((1, D), lambda t: (0, 0)), mod, mod],
        out_specs=[row, row],
        out_shape=[jax.ShapeDtypeStruct((M, D), F32), jax.ShapeDtypeStruct((M, D), BF16)],
        compiler_params=_params("arbitrary"),
        name="resid_norm",
    )(x, y, w, normw_next, modtile, modtile_next)


def _resid_kernel(x_ref, y_ref, w_ref, mod_ref, xo_ref):
    xo_ref[...] = _ffn_residual(x_ref, y_ref, w_ref, mod_ref)


def _resid(x, y, w, modtile):
    M, D = x.shape
    row = pl.BlockSpec((ROW_TILE, D), lambda t: (t, 0))
    return pl.pallas_call(
        _resid_kernel,
        grid=(M // ROW_TILE,),
        in_specs=[row, pl.BlockSpec((ROW_TILE, TOP_K * D), lambda t: (t, 0)),
                  pl.BlockSpec((ROW_TILE, LANES), lambda t: (t, 0)),
                  pl.BlockSpec((None, 8, D), lambda t: (t, 0, 0))],
        out_specs=row,
        out_shape=jax.ShapeDtypeStruct((M, D), F32),
        compiler_params=_params("arbitrary"),
        name="resid",
    )(x, y, w, modtile)


def _head_rms(o, width, normw):
    parts = []
    for h in range(o.shape[1] // width):
        oh = o[:, h * width:(h + 1) * width]
        parts.append(oh * lax.rsqrt(jnp.mean(oh * oh, axis=-1, keepdims=True) + NORM_EPS))
    return jnp.concatenate(parts, axis=1) * normw


def _finish_lru(gate_ref, hf_ref, hb_ref):
    return jax.nn.gelu(gate_ref[...]) * (hf_ref[...] + hb_ref[...])


def _finish_dn(z_ref, of_ref, ob_ref, nw_ref):
    return _head_rms(of_ref[...] + ob_ref[...], DN_DV, nw_ref[...]) * _silu(z_ref[...])


def _finish_ret(gate_ref, of_ref, ob_ref, nw_ref):
    return _head_rms(of_ref[...] + ob_ref[...], RET_DV, nw_ref[...]) * _silu(gate_ref[...])


def _finish_att(o_ref):
    return o_ref[...]


def _outproj_kernel(finish, n_fin, *refs):
    fin_refs = refs[:n_fin]
    w_ref, x_ref, nw_ref, mod_ref, rw_ref, xo_ref, h_ref, lg_ref = refs[n_fin:]
    a = finish(*fin_refs).astype(BF16)
    y = jnp.dot(a, w_ref[...], preferred_element_type=F32)
    x = x_ref[...] + mod_ref[2:3, :] * y
    xo_ref[...] = x
    h = _norm_mod(x, nw_ref[...], mod_ref[3:4, :], mod_ref[4:5, :])
    h_ref[...] = h.astype(BF16)
    lg_ref[...] = jnp.dot(h, rw_ref[...], preferred_element_type=F32, precision=lax.Precision.HIGHEST)


def _outproj(finish, fin_args, fin_specs, w_bf16, x, normw_ffn, modtile, router_w_pad, tm):
    M, D = x.shape
    K = w_bf16.shape[0]
    per = ROW_TILE // tm
    row = pl.BlockSpec((tm, D), lambda t: (t, 0))
    in_specs = list(fin_specs) + [
        pl.BlockSpec((K, D), lambda t: (0, 0)),
        row,
        pl.BlockSpec((1, D), lambda t: (0, 0)),
        pl.BlockSpec((None, 8, D), lambda t: (t // per, 0, 0)),
        pl.BlockSpec((D, LANES), lambda t: (0, 0)),
    ]
    return pl.pallas_call(
        functools.partial(_outproj_kernel, finish, len(fin_args)),
        grid=(M // tm,),
        in_specs=in_specs,
        out_specs=[row, row, pl.BlockSpec((tm, LANES), lambda t: (t, 0))],
        out_shape=[jax.ShapeDtypeStruct((M, D), F32), jax.ShapeDtypeStruct((M, D), BF16),
                   jax.ShapeDtypeStruct((M, LANES), F32)],
        compiler_params=_params("arbitrary"),
        name="outproj",
    )(*fin_args, w_bf16, x, normw_ffn, modtile, router_w_pad)


def _rope_tables(n_ctx, seq, head_dim):
    quarter = head_dim // 4
    pos = jnp.arange(seq)
    inv_freq = ROPE_BASE ** (-jnp.arange(quarter, dtype=F32) / quarter)
    ang_r = (pos // GRID_W).astype(F32)[:, None] * inv_freq
    ang_c = (pos % GRID_W).astype(F32)[:, None] * inv_freq
    c = jnp.concatenate([jnp.cos(ang_r), jnp.cos(ang_r), jnp.cos(ang_c), jnp.cos(ang_c)], axis=1)
    s = jnp.concatenate([-jnp.sin(ang_r), jnp.sin(ang_r), -jnp.sin(ang_c), jnp.sin(ang_c)], axis=1)
    c = jnp.concatenate([jnp.ones((n_ctx, head_dim), F32), c], axis=0)
    s = jnp.concatenate([jnp.zeros((n_ctx, head_dim), F32), s], axis=0)
    return c, s


def _rope(x, c, s, quarter):
    width = x.shape[1]
    if 2 * quarter == LANES:
        parts = [pltpu.roll(x[:, j:j + LANES], quarter, axis=1) for j in range(0, width, LANES)]
        partner = parts[0] if len(parts) == 1 else jnp.concatenate(parts, axis=1)
    else:
        assert width == LANES and 4 * quarter == LANES
        lane = lax.broadcasted_iota(jnp.int32, x.shape, 1)
        partner = jnp.where((lane % (2 * quarter)) < quarter,
                            pltpu.roll(x, LANES - quarter, axis=1), pltpu.roll(x, quarter, axis=1))
    return x * c + partner * s


ATT_TILE = 128


def _att_prep_kernel(p_ref, qn_ref, kn_ref, c_ref, s_ref, q_ref, k_ref, v_ref):
    c = c_ref[...]
    s = s_ref[...]
    qw = ATT_HEADS * ATT_HD
    kw = ATT_KV_HEADS * ATT_HD

    def norm_rope(xh, w):
        y = xh * lax.rsqrt(jnp.mean(xh * xh, axis=-1, keepdims=True) + NORM_EPS) * w
        return _rope(y, c, s, ATT_HD // 4)

    for h in range(ATT_HEADS):
        qh = norm_rope(p_ref[:, h * ATT_HD:(h + 1) * ATT_HD], qn_ref[...])
        q_ref[:, h * ATT_HD:(h + 1) * ATT_HD] = (qh * ATT_HD ** -0.5).astype(BF16)
    for h in range(ATT_KV_HEADS):
        kh = norm_rope(p_ref[:, qw + h * ATT_HD:qw + (h + 1) * ATT_HD], kn_ref[...])
        k_ref[:, h * ATT_HD:(h + 1) * ATT_HD] = kh.astype(BF16)
    v_ref[...] = p_ref[:, qw + kw:qw + 2 * kw].astype(BF16)


def _att_prep(p, q_norm, k_norm, rope_c, rope_s, tiles_per_batch):
    M = p.shape[0]
    qw, kw = ATT_HEADS * ATT_HD, ATT_KV_HEADS * ATT_HD
    tab = pl.BlockSpec((ROW_TILE, ATT_HD), lambda t: (t % tiles_per_batch, 0))
    return pl.pallas_call(
        _att_prep_kernel,
        grid=(M // ROW_TILE,),
        in_specs=[pl.BlockSpec((ROW_TILE, qw + 2 * kw), lambda t: (t, 0)),
                  pl.BlockSpec((1, ATT_HD), lambda t: (0, 0)),
                  pl.BlockSpec((1, ATT_HD), lambda t: (0, 0)), tab, tab],
        out_specs=[pl.BlockSpec((ROW_TILE, qw), lambda t: (t, 0)),
                   pl.BlockSpec((ROW_TILE, kw), lambda t: (t, 0)),
                   pl.BlockSpec((ROW_TILE, kw), lambda t: (t, 0))],
        out_shape=[jax.ShapeDtypeStruct((M, qw), BF16), jax.ShapeDtypeStruct((M, kw), BF16),
                   jax.ShapeDtypeStruct((M, kw), BF16)],
        compiler_params=_params("arbitrary"),
        name="att_prep",
    )(p, q_norm, k_norm, rope_c, rope_s)


def _att_kernel(n_ctx_tiles, n_tiles, q_ref, kp_ref, ko_ref, kn_ref, vp_ref, vo_ref, vn_ref,
                kc_ref, vc_ref, sink_ref, o_ref):
    t = pl.program_id(1)
    T = ATT_TILE
    rows = ATT_G * T
    row = lax.broadcasted_iota(jnp.int32, (rows, 3 * T), 0) % T
    col = lax.broadcasted_iota(jnp.int32, (rows, 3 * T), 1)
    blk = col // T
    c = col % T
    latent = t >= n_ctx_tiles
    prev_ok = latent & (t >= n_ctx_tiles + 1)
    next_ok = latent & (t <= n_tiles - 2)
    band = ((blk == 1) & latent) | ((blk == 0) & (c >= row) & prev_ok) | ((blk == 2) & (c <= row) & next_ok)
    nt = (((1,), (1,)), ((), ()))
    for kvh in range(ATT_KV_HEADS):
        ks = slice(kvh * ATT_HD, (kvh + 1) * ATT_HD)
        kwin = jnp.concatenate([kp_ref[:, ks], ko_ref[:, ks], kn_ref[:, ks]], axis=0)
        vwin = jnp.concatenate([vp_ref[:, ks], vo_ref[:, ks], vn_ref[:, ks]], axis=0)
        q4 = jnp.concatenate([q_ref[:, (kvh * ATT_G + g) * ATT_HD:(kvh * ATT_G + g + 1) * ATT_HD]
                              for g in range(ATT_G)], axis=0)
        s_loc = lax.dot_general(q4, kwin, nt, preferred_element_type=F32)
        s_loc = jnp.where(band, s_loc, -jnp.inf)
        s_ctx = lax.dot_general(q4, kc_ref[:, ks], nt, preferred_element_type=F32)
        sink = sink_ref[kvh][:, 0:1]
        m = jnp.maximum(sink, jnp.maximum(jnp.max(s_loc, axis=-1, keepdims=True),
                                          jnp.max(s_ctx, axis=-1, keepdims=True)))
        p_loc = jnp.exp(s_loc - m)
        p_ctx = jnp.exp(s_ctx - m)
        den = jnp.exp(sink - m) + jnp.sum(p_loc, axis=-1, keepdims=True) + jnp.sum(p_ctx, axis=-1, keepdims=True)
        o = jnp.dot(p_loc.astype(BF16), vwin, preferred_element_type=F32)
        o = o + jnp.dot(p_ctx.astype(BF16), vc_ref[:, ks], preferred_element_type=F32)
        o = o / den
        for g in range(ATT_G):
            h = kvh * ATT_G + g
            o_ref[:, h * ATT_HD:(h + 1) * ATT_HD] = o[g * T:(g + 1) * T, :].astype(BF16)


def _attention(q, k, v, sink, n_batch, n_ctx, seq):
    M, qw = q.shape
    kw = k.shape[1]
    T = ATT_TILE
    n_tiles = (n_ctx + seq) // T
    n_ctx_tiles = n_ctx // T
    sink_rows = jnp.broadcast_to(
        jnp.repeat(sink.astype(F32).reshape(ATT_KV_HEADS, ATT_G), T, axis=1)[:, :, None],
        (ATT_KV_HEADS, ATT_G * T, LANES))

    def tile(off):
        return pl.BlockSpec((T, kw), lambda b, t: (b * n_tiles + jnp.clip(t + off, 0, n_tiles - 1), 0))

    ctx = pl.BlockSpec((n_ctx, kw), lambda b, t: (b * (n_tiles // n_ctx_tiles), 0))
    return pl.pallas_call(
        functools.partial(_att_kernel, n_ctx_tiles, n_tiles),
        grid=(n_batch, n_tiles),
        in_specs=[pl.BlockSpec((T, qw), lambda b, t: (b * n_tiles + t, 0)),
                  tile(-1), tile(0), tile(1), tile(-1), tile(0), tile(1), ctx, ctx,
                  pl.BlockSpec((ATT_KV_HEADS, ATT_G * T, LANES), lambda b, t: (0, 0, 0))],
        out_specs=pl.BlockSpec((T, qw), lambda b, t: (b * n_tiles + t, 0)),
        out_shape=jax.ShapeDtypeStruct((M, qw), BF16),
        compiler_params=_params("arbitrary", "arbitrary"),
        name="window_attention",
    )(q, k, k, k, v, v, v, k, v, sink_rows)


def _chunk_of_step(s, n_chunks, n_ctx_chunks, reverse):
    if not reverse:
        return s
    return jnp.where(s < n_ctx_chunks, n_ctx_chunks - 1 - s, n_chunks - 1 - (s - n_ctx_chunks))


RET_CHUNK = ROW_TILE


def _ret_kernel(reverse, q_ref, k_ref, v_ref, c_ref, s_ref, o_ref, state_ref):
    @pl.when(pl.program_id(1) == 0)
    def _():
        state_ref[...] = jnp.zeros_like(state_ref)

    C = RET_CHUNK
    c = c_ref[...]
    s = s_ref[...]
    i = lax.broadcasted_iota(jnp.int32, (C, C), 0)
    j = lax.broadcasted_iota(jnp.int32, (C, C), 1)
    pos = lax.broadcasted_iota(jnp.int32, (C, 1), 0).astype(F32)
    diff = (j - i) if reverse else (i - j)
    mask = diff > 0 if reverse else diff >= 0
    dist = jnp.maximum(diff, 0).astype(F32)
    steps_in = (C - pos) if reverse else (pos + 1.0)
    steps_out = pos if reverse else (C - 1.0 - pos)
    nt = (((1,), (1,)), ((), ()))
    for h in range(RET_HEADS):
        log_gamma = math.log1p(-2.0 ** (-5.0 - h))
        q = _rope(q_ref[:, h * RET_DK:(h + 1) * RET_DK], c, s, RET_DK // 4)
        k = _rope(k_ref[:, h * RET_DK:(h + 1) * RET_DK], c, s, RET_DK // 4) * RET_DK ** -0.5
        v = v_ref[:, h * RET_DV:(h + 1) * RET_DV].astype(BF16)
        qb = q.astype(BF16)
        dmat = jnp.where(mask, jnp.exp(dist * log_gamma), 0.0)
        a = lax.dot_general(qb, k.astype(BF16), nt, preferred_element_type=F32) * dmat
        st = state_ref[h]
        o = jnp.dot(a.astype(BF16), v, preferred_element_type=F32)
        o = o + jnp.dot(qb, st.astype(BF16), preferred_element_type=F32) * jnp.exp(steps_in * log_gamma)
        o_ref[:, h * RET_DV:(h + 1) * RET_DV] = o
        kt = (k * jnp.exp(steps_out * log_gamma)).T.astype(BF16)
        state_ref[h] = st * math.exp(C * log_gamma) + jnp.dot(kt, v, preferred_element_type=F32)


def _retention(p, rope_c, rope_s, n_batch, n_ctx, seq, reverse):
    M = p.shape[0]
    C = RET_CHUNK
    n_chunks = (n_ctx + seq) // C
    n_ctx_chunks = n_ctx // C
    hk, hv = RET_HEADS * RET_DK, RET_HEADS * RET_DV

    def chunk(b, s):
        return _chunk_of_step(s, n_chunks, n_ctx_chunks, reverse)

    tab = pl.BlockSpec((C, RET_DK), lambda b, s: (chunk(b, s), 0))
    return pl.pallas_call(
        functools.partial(_ret_kernel, reverse),
        grid=(n_batch, n_chunks),
        in_specs=[pl.BlockSpec((C, hk), lambda b, s: (b * n_chunks + chunk(b, s), 0)),
                  pl.BlockSpec((C, hk), lambda b, s: (b * n_chunks + chunk(b, s), 1)),
                  pl.BlockSpec((C, hv), lambda b, s: (b * n_chunks + chunk(b, s), 1)),
                  tab, tab],
        out_specs=pl.BlockSpec((C, hv), lambda b, s: (b * n_chunks + chunk(b, s), 0)),
        out_shape=jax.ShapeDtypeStruct((M, hv), F32),
        scratch_shapes=[pltpu.VMEM((RET_HEADS, RET_DK, RET_DV), F32)],
        compiler_params=_params("arbitrary", "arbitrary"),
        name="retention_rev" if reverse else "retention_fwd",
    )(p, p, p, rope_c, rope_s)


LRU_CHUNK = ROW_TILE


def _lru_kernel(reverse, n_chunks, n_ctx_chunks, x_ref, xp_ref, xn_ref, cw_ref, cb_ref, wa_ref, ba_ref,
                wx_ref, bx_ref, lam_ref, o_ref, h_ref):
    step = pl.program_id(1)

    @pl.when(step == 0)
    def _():
        h_ref[...] = jnp.zeros_like(h_ref)

    C = LRU_CHUNK
    W = LRU_BLOCK
    chunk = _chunk_of_step(step, n_chunks, n_ctx_chunks, reverse)
    has_prev = ((chunk != 0) & (chunk != n_ctx_chunks)).astype(F32)
    has_next = ((chunk != n_ctx_chunks - 1) & (chunk != n_chunks - 1)).astype(F32)
    rows = C + 2 * HALO
    sub = lax.broadcasted_iota(jnp.int32, (C, W), 0) % SUBLANES

    def block(n, carry):
        lanes = pl.ds(pl.multiple_of(n * W, W), W)
        full = jnp.concatenate([xp_ref[:, lanes] * has_prev, x_ref[:, lanes], xn_ref[:, lanes] * has_next], axis=0)
        cw = cw_ref[:, lanes]
        xc = (cw[0:1] * pltpu.roll(full, 2, axis=0)[HALO:HALO + C]
              + cw[1:2] * pltpu.roll(full, 1, axis=0)[HALO:HALO + C]
              + cw[2:3] * full[HALO:HALO + C]
              + cw[3:4] * pltpu.roll(full, rows - 1, axis=0)[HALO:HALO + C]) + cb_ref[:, lanes]
        xb = xc.astype(BF16)
        r = _sigmoid(jnp.dot(xb, wa_ref[n], preferred_element_type=F32) + ba_ref[:, lanes])
        gi = _sigmoid(jnp.dot(xb, wx_ref[n], preferred_element_type=F32) + bx_ref[:, lanes])
        a = jnp.exp((-LRU_C) * r * _softplus(-lam_ref[:, lanes]))
        b = jnp.sqrt(jnp.maximum(1.0 - a * a, 0.0)) * (gi * xc)
        for k in (1, 2, 4):
            if reverse:
                keep = sub < SUBLANES - k
                a_sh = pltpu.roll(a, C - k, axis=0)
                b_sh = pltpu.roll(b, C - k, axis=0)
            else:
                keep = sub >= k
                a_sh = pltpu.roll(a, k, axis=0)
                b_sh = pltpu.roll(b, k, axis=0)
            b = b + a * jnp.where(keep, b_sh, 0.0)
            a = a * jnp.where(keep, a_sh, 1.0)
        hin = h_ref[0:1, lanes]
        groups = C // SUBLANES
        outs = [None] * groups
        for g in (range(groups - 1, -1, -1) if reverse else range(groups)):
            lo = g * SUBLANES
            hg = b[lo:lo + SUBLANES] + a[lo:lo + SUBLANES] * hin
            outs[g] = hg
            hin = hg[0:1] if reverse else hg[SUBLANES - 1:SUBLANES]
        o_ref[:, lanes] = jnp.concatenate(outs, axis=0)
        h_ref[:, lanes] = jnp.broadcast_to(hin, (SUBLANES, W))
        return carry

    lax.fori_loop(0, LRU_BLOCKS, block, 0)


def _rglru(p, conv_w, conv_b, gate_a_w, gate_a_b, gate_x_w, gate_x_b, lam, n_batch, n_ctx, seq, reverse):
    M = p.shape[0]
    C = LRU_CHUNK
    Wd = LRU_WIDTH
    n_chunks = (n_ctx + seq) // C
    n_ctx_chunks = n_ctx // C
    per = C // HALO
    last = M // HALO - 1

    def tile(b, s):
        return b * n_chunks + _chunk_of_step(s, n_chunks, n_ctx_chunks, reverse)

    vec = pl.BlockSpec((1, Wd), lambda b, s: (0, 0))
    wts = pl.BlockSpec((LRU_BLOCKS, LRU_BLOCK, LRU_BLOCK), lambda b, s: (0, 0, 0))
    return pl.pallas_call(
        functools.partial(_lru_kernel, reverse, n_chunks, n_ctx_chunks),
        grid=(n_batch, n_chunks),
        in_specs=[pl.BlockSpec((C, Wd), lambda b, s: (tile(b, s), 1)),
                  pl.BlockSpec((HALO, Wd), lambda b, s: (jnp.maximum(tile(b, s) * per - 1, 0), 1)),
                  pl.BlockSpec((HALO, Wd), lambda b, s: (jnp.minimum((tile(b, s) + 1) * per, last), 1)),
                  pl.BlockSpec((SUBLANES, Wd), lambda b, s: (0, 0)),
                  vec, wts, vec, wts, vec, vec],
        out_specs=pl.BlockSpec((C, Wd), lambda b, s: (tile(b, s), 0)),
        out_shape=jax.ShapeDtypeStruct((M, Wd), F32),
        scratch_shapes=[pltpu.VMEM((SUBLANES, Wd), F32)],
        compiler_params=_params("arbitrary", "arbitrary"),
        name="rglru_rev" if reverse else "rglru_fwd",
    )(p, p, p, conv_w, conv_b, gate_a_w, gate_a_b, gate_x_w, gate_x_b, lam)


DN_CHUNK = 128
DN_HEAD_UNROLL = 4
DN_INV_BASE = SUBLANES
DN_NEUMANN_ROUNDS = 2


def _dn_kernel(reverse, d, n_chunks, n_ctx_chunks, q_ref, k_ref, v_ref, xp_ref, xn_ref, tail_ref, cw_ref,
               alog_ref, dtb_ref, o_ref, state_ref, gct_ref, bt_ref):
    step = pl.program_id(1)

    @pl.when(step == 0)
    def _():
        state_ref[...] = jnp.zeros_like(state_ref)

    C = DN_CHUNK
    H = DN_HEADS
    HK = H * DN_DK
    chunk = _chunk_of_step(step, n_chunks, n_ctx_chunks, reverse)
    has_prev = ((chunk != 0) & (chunk != n_ctx_chunks)).astype(F32)
    has_next = ((chunk != n_ctx_chunks - 1) & (chunk != n_chunks - 1)).astype(F32)
    rows = C + 2 * HALO
    ci = lax.broadcasted_iota(jnp.int32, (C, C), 0)
    si = lax.broadcasted_iota(jnp.int32, (C, C), 1)
    mask = (si >= ci) if reverse else (si <= ci)
    strict = (si > ci) if reverse else (si < ci)
    eye = (si == ci).astype(F32)
    last = 0 if reverse else C - 1
    same_base = (ci // DN_INV_BASE) == (si // DN_INV_BASE)
    merge_masks = []
    size = DN_INV_BASE
    while size < C:
        cb, sb = ci // size, si // size
        merge_masks.append(((cb % 2 == 0) & (sb == cb + 1)) if reverse else ((cb % 2 == 1) & (sb == cb - 1)))
        size *= 2

    tail = tail_ref[...]
    beta = _sigmoid(tail)
    g = -jnp.exp(alog_ref[...]) * _softplus(tail + dtb_ref[...])
    gcum = jnp.dot(mask.astype(F32), g, preferred_element_type=F32, precision=lax.Precision.HIGHEST)
    gct_ref[...] = gcum.T
    bt_ref[...] = beta.T
    nt = (((1,), (1,)), ((), ()))

    def conv_silu(ref, ref_off, h):
        lanes_in = pl.ds(pl.multiple_of(h * DN_DK, DN_DK), DN_DK)
        lanes_all = pl.ds(pl.multiple_of(ref_off + h * DN_DK, DN_DK), DN_DK)
        full = jnp.concatenate([xp_ref[:, lanes_all] * has_prev, ref[:, lanes_in], xn_ref[:, lanes_all] * has_next],
                               axis=0)
        cw = cw_ref[:, lanes_all]
        y = (cw[0:1] * pltpu.roll(full, 2, axis=0)[HALO:HALO + C]
             + cw[1:2] * pltpu.roll(full, 1, axis=0)[HALO:HALO + C]
             + cw[2:3] * full[HALO:HALO + C]
             + cw[3:4] * pltpu.roll(full, rows - 1, axis=0)[HALO:HALO + C])
        return _silu(y)

    def l2n(x):
        return x * lax.rsqrt(jnp.sum(x * x, axis=-1, keepdims=True) + NORM_EPS)

    def head(h, carry):
        g_row = gct_ref[pl.ds(2 * H + d * H + h, 1), :]
        g_rows = jnp.broadcast_to(g_row, (C, C))
        g_cols = g_rows.T
        b_cols = jnp.broadcast_to(bt_ref[pl.ds(d * H + h, 1), :], (C, C)).T
        g_last = g_row[:, last:last + 1]
        decay = jnp.where(mask, jnp.exp(jnp.where(mask, g_cols - g_rows, 0.0)), 0.0)

        q = l2n(conv_silu(q_ref, 0, h)) * DN_DK ** -0.5
        k = l2n(conv_silu(k_ref, HK, h))
        v = conv_silu(v_ref, 2 * HK, h)
        kb = k * b_cols
        k16 = k.astype(BF16)
        m = jnp.where(strict, lax.dot_general(kb.astype(BF16), k16, nt, preferred_element_type=F32) * decay, 0.0)
        a_intra = jnp.where(mask, lax.dot_general(q.astype(BF16), k16, nt, preferred_element_type=F32) * decay, 0.0)

        m16 = m.astype(BF16)
        n = jnp.where(same_base, -m, 0.0)
        t = eye + n
        for _ in range(DN_NEUMANN_ROUNDS):
            n16 = n.astype(BF16)
            n = jnp.dot(n16, n16, preferred_element_type=F32)
            t = t + jnp.dot(t.astype(BF16), n.astype(BF16), preferred_element_type=F32)
        for pair in merge_masks:
            t16 = t.astype(BF16)
            x = jnp.dot(t16, jnp.where(pair, m16, jnp.zeros_like(m16)), preferred_element_type=F32)
            t = t - jnp.dot(x.astype(BF16), t16, preferred_element_type=F32)
        rhs = jnp.concatenate([v * b_cols, kb * jnp.exp(g_cols)], axis=1).astype(BF16)
        sol = jnp.dot(t.astype(BF16), rhs, preferred_element_type=F32)
        u, w = sol[:, :DN_DV], sol[:, DN_DV:]

        st = state_ref[h]
        wq = jnp.concatenate([w, q * jnp.exp(g_cols)], axis=0).astype(BF16)
        r = jnp.dot(wq, st.astype(BF16), preferred_element_type=F32)
        v_new = u - r[:C]
        v16 = v_new.astype(BF16)
        o = r[C:] + jnp.dot(a_intra.astype(BF16), v16, preferred_element_type=F32)
        o_ref[:, pl.ds(pl.multiple_of(h * DN_DV, DN_DV), DN_DV)] = o
        kt = (k * jnp.exp(g_last - g_cols)).T.astype(BF16)
        state_ref[h] = st * jnp.exp(g_last) + jnp.dot(kt, v16, preferred_element_type=F32)
        return carry

    lax.fori_loop(0, H, head, 0, unroll=DN_HEAD_UNROLL)


def _deltanet(p, tail, conv_w, alog_row, dtb_row, n_batch, n_ctx, seq, d):
    M = p.shape[0]
    C = DN_CHUNK
    reverse = d == 1
    n_chunks = (n_ctx + seq) // C
    n_ctx_chunks = n_ctx // C
    HK = DN_HEADS * DN_DK
    per = C // HALO
    last = M // HALO - 1

    def tile(b, s):
        return b * n_chunks + _chunk_of_step(s, n_chunks, n_ctx_chunks, reverse)

    row1 = pl.BlockSpec((1, LANES), lambda b, s: (0, 0))
    return pl.pallas_call(
        functools.partial(_dn_kernel, reverse, d, n_chunks, n_ctx_chunks),
        grid=(n_batch, n_chunks),
        in_specs=[pl.BlockSpec((C, HK), lambda b, s: (tile(b, s), 0)),
                  pl.BlockSpec((C, HK), lambda b, s: (tile(b, s), 1)),
                  pl.BlockSpec((C, HK), lambda b, s: (tile(b, s), 2)),
                  pl.BlockSpec((HALO, 4 * HK), lambda b, s: (jnp.maximum(tile(b, s) * per - 1, 0), 0)),
                  pl.BlockSpec((HALO, 4 * HK), lambda b, s: (jnp.minimum((tile(b, s) + 1) * per, last), 0)),
                  pl.BlockSpec((C, LANES), lambda b, s: (tile(b, s), 0)),
                  pl.BlockSpec((SUBLANES, 3 * HK), lambda b, s: (0, 0)),
                  row1, row1],
        out_specs=pl.BlockSpec((C, HK), lambda b, s: (tile(b, s), 0)),
        out_shape=jax.ShapeDtypeStruct((M, HK), F32),
        scratch_shapes=[pltpu.VMEM((DN_HEADS, DN_DK, DN_DV), F32),
                        pltpu.VMEM((LANES, C), F32), pltpu.VMEM((LANES, C), F32)],
        compiler_params=_params("arbitrary", "arbitrary"),
        name="deltanet_rev" if reverse else "deltanet_fwd",
    )(p, p, p, p, p, tail, conv_w, alog_row, dtb_row)


def _moe_kernel(be_ref, xs_ref, wg_ref, wu_ref, wd_ref, o_ref, wgb, wub, wdb):
    b = pl.program_id(0)
    e = be_ref[b]
    prev = be_ref[jnp.maximum(b - 1, 0)]

    @pl.when((b == 0) | (e != prev))
    def _():
        wgb[...] = wg_ref[...].astype(BF16)
        wub[...] = wu_ref[...].astype(BF16)
        wdb[...] = wd_ref[...].astype(BF16)

    x = xs_ref[...]
    g = jnp.dot(x, wgb[...], preferred_element_type=F32)
    u = jnp.dot(x, wub[...], preferred_element_type=F32)
    a = (_silu(g) * u).astype(BF16)
    o_ref[...] = jnp.dot(a, wdb[...], preferred_element_type=F32)


def _moe_experts(xs, block_expert, w_gate, w_up, w_down, layer):
    n_slots, D = xs.shape
    n_blocks = n_slots // MOE_BLOCK
    DE = w_gate.shape[-1]
    grid_spec = pltpu.PrefetchScalarGridSpec(
        num_scalar_prefetch=1,
        grid=(n_blocks,),
        in_specs=[pl.BlockSpec((MOE_BLOCK, D), lambda b, be: (b, 0)),
                  pl.BlockSpec((None, None, D, DE), lambda b, be: (layer, be[b], 0, 0)),
                  pl.BlockSpec((None, None, D, DE), lambda b, be: (layer, be[b], 0, 0)),
                  pl.BlockSpec((None, None, DE, D), lambda b, be: (layer, be[b], 0, 0))],
        out_specs=pl.BlockSpec((MOE_BLOCK, D), lambda b, be: (b, 0)),
        scratch_shapes=[pltpu.VMEM((D, DE), BF16), pltpu.VMEM((D, DE), BF16), pltpu.VMEM((DE, D), BF16)],
    )
    return pl.pallas_call(
        _moe_kernel,
        grid_spec=grid_spec,
        out_shape=jax.ShapeDtypeStruct((n_slots, D), F32),
        compiler_params=_params("arbitrary"),
        name="moe_experts",
    )(block_expert.astype(jnp.int32), xs, w_gate, w_up, w_down)


def _route(logits, router_b):
    n = logits.shape[0]
    scores = jax.nn.sigmoid(logits)
    biased = (scores + router_b.astype(F32)).reshape(n, N_GROUPS, EXPERTS_PER_GROUP)
    group_score = jnp.sum(lax.top_k(biased, TOP_K)[0], axis=-1)
    group = jnp.argmax(group_score, axis=-1)
    in_group = jnp.take_along_axis(biased, group[:, None, None], axis=1)[:, 0]
    expert_ids = group[:, None] * EXPERTS_PER_GROUP + lax.top_k(in_group, TOP_K)[1]
    gate = jnp.take_along_axis(scores, expert_ids, axis=1)
    return expert_ids, gate / jnp.sum(gate, axis=-1, keepdims=True)


def _moe(h2, logits, router_b, w_gate, w_up, w_down, layer):
    N, D = h2.shape
    expert_ids, weights = _route(logits[:, :N_EXPERTS], router_b)
    NK = N * TOP_K
    flat_e = expert_ids.reshape(NK).astype(jnp.int32)
    onehot = (flat_e[:, None] == jnp.arange(N_EXPERTS, dtype=jnp.int32)[None, :]).astype(jnp.int32)
    ranks = jnp.cumsum(onehot, axis=0)
    counts = ranks[-1]
    rank = jnp.sum(ranks * onehot, axis=1) - 1
    padded = (counts + MOE_BLOCK - 1) // MOE_BLOCK * MOE_BLOCK
    pad_end = jnp.cumsum(padded)
    pad_start = pad_end - padded
    dest = pad_start[flat_e] + rank
    n_blocks = (NK + N_EXPERTS * (MOE_BLOCK - 1)) // MOE_BLOCK
    n_slots = n_blocks * MOE_BLOCK
    slot_tok = jnp.zeros((n_slots,), jnp.int32).at[dest].set(jnp.arange(NK, dtype=jnp.int32) // TOP_K)
    block_expert = jnp.minimum(jnp.searchsorted(pad_end, jnp.arange(n_blocks) * MOE_BLOCK, side='right'),
                               N_EXPERTS - 1)
    xs = h2[slot_tok]
    ys = _moe_experts(xs, block_expert, w_gate, w_up, w_down, layer)
    y = ys[dest].reshape(N, TOP_K * D)
    w = jnp.pad(weights.astype(F32), ((0, 0), (0, LANES - TOP_K)))
    return y, w


def _tile_mod(mod_rows, n_batch, tiles_per_batch, n_ctx_tiles):
    D = mod_rows.shape[1] // 6
    t = jnp.arange(n_batch * tiles_per_batch)
    src = jnp.where(t % tiles_per_batch < n_ctx_tiles, n_batch, t // tiles_per_batch)
    table = mod_rows.reshape(mod_rows.shape[0], 6, D)[src]
    return jnp.pad(table, ((0, 0), (0, 2), (0, 0)))


def kernel(x, c, ctx, c_ctx, mod_w, mod_b, norm_mix, norm_ffn, lru_w_in, lru_conv_w, lru_conv_b, lru_gate_a_w, lru_gate_a_b, lru_gate_x_w, lru_gate_x_b, lru_lambda, lru_w_out, dn_w_in, dn_conv_w, dn_a_log, dn_dt_bias, dn_norm, dn_w_out, ret_w_in, ret_norm, ret_w_out, att_w_in, att_q_norm, att_k_norm, att_sink, att_w_out, router_w, router_b, moe_w_gate, moe_w_up, moe_w_down):
    Bb, S, D = x.shape
    n_ctx = ctx.shape[1]
    TB = n_ctx + S
    M = Bb * TB
    tiles_per_batch = TB // ROW_TILE
    assert n_ctx % ROW_TILE == 0 and S % ROW_TILE == 0 and Bb < 16

    xa = jnp.concatenate([ctx, x], axis=1).reshape(M, D)
    cond = jnp.zeros((16, D), F32).at[:Bb].set(jax.nn.silu(c)).at[Bb].set(jax.nn.silu(c_ctx)).astype(BF16)
    modtiles = [_tile_mod(_matmul(cond, mod_w, layer=i) + mod_b[i], Bb, tiles_per_batch, n_ctx // ROW_TILE)
                for i in range(DEPTH)]
    router_w_pad = jnp.pad(router_w.astype(F32), ((0, 0), (0, LANES - N_EXPERTS)))
    rope_att = _rope_tables(n_ctx, S, ATT_HD)
    rope_ret = _rope_tables(n_ctx, S, RET_DK)

    def row(width, col=0, dtype_rows=ROW_TILE):
        return pl.BlockSpec((dtype_rows, width), lambda t: (t, col))

    def vec(width):
        return pl.BlockSpec((1, width), lambda t: (0, 0))

    h = _prenorm(xa, norm_mix[0][None], modtiles[0])
    for i in range(DEPTH):
        kind, j = i % N_MIXERS, i // N_MIXERS
        tm = ROW_TILE
        if kind == 0:
            p = _matmul(h, lru_w_in, layer=j)
            cw = jnp.pad(lru_conv_w[j], ((0, SUBLANES - CONV_W), (0, 0)))
            dirs = [_rglru(p, cw, lru_conv_b[j][None], lru_gate_a_w[j, d].astype(BF16), lru_gate_a_b[j, d][None],
                           lru_gate_x_w[j, d].astype(BF16), lru_gate_x_b[j, d][None], lru_lambda[j, d][None],
                           Bb, n_ctx, S, reverse=(d == 1)) for d in range(2)]
            fin = (_finish_lru, [p] + dirs, [row(LRU_WIDTH, 0), row(LRU_WIDTH), row(LRU_WIDTH)])
            w_out = lru_w_out[j]
        elif kind == 1:
            HK = DN_HEADS * DN_DK
            p = _matmul(h, dn_w_in, layer=j, n_out=4 * HK)
            tail = _matmul(h, jnp.pad(dn_w_in[j][:, 4 * HK:], ((0, 0), (0, LANES - 4 * DN_HEADS))))
            cw = jnp.pad(dn_conv_w[j], ((0, SUBLANES - CONV_W), (0, 0)))
            alog_row = jnp.zeros((1, LANES), F32).at[0, 2 * DN_HEADS:4 * DN_HEADS].set(dn_a_log[j].reshape(-1))
            dtb_row = jnp.zeros((1, LANES), F32).at[0, 2 * DN_HEADS:4 * DN_HEADS].set(dn_dt_bias[j].reshape(-1))
            dirs = [_deltanet(p, tail, cw, alog_row, dtb_row, Bb, n_ctx, S, d) for d in range(2)]
            fin = (_finish_dn, [p] + dirs + [jnp.tile(dn_norm[j], DN_HEADS)[None]],
                   [row(HK, 3), row(HK), row(HK), vec(HK)])
            w_out = dn_w_out[j]
        elif kind == 2:
            tm = ROW_TILE // 2
            HV = RET_HEADS * RET_DV
            p = _matmul(h, ret_w_in, layer=j)
            dirs = [_retention(p, rope_ret[0], rope_ret[1], Bb, n_ctx, S, reverse=(d == 1)) for d in range(2)]
            fin = (_finish_ret, [p] + dirs + [ret_norm[j][None]],
                   [row(HV, 2, tm), row(HV, 0, tm), row(HV, 0, tm), vec(HV)])
            w_out = ret_w_out[j]
        else:
            p = _matmul(h, att_w_in, layer=j)
            q, k, v = _att_prep(p, att_q_norm[j][None], att_k_norm[j][None], rope_att[0], rope_att[1],
                                tiles_per_batch)
            o = _attention(q, k, v, att_sink[j], Bb, n_ctx, S)
            fin = (_finish_att, [o], [row(ATT_HEADS * ATT_HD)])
            w_out = att_w_out[j]

        xa, h2, logits = _outproj(fin[0], fin[1], fin[2], w_out.astype(BF16), xa, norm_ffn[i][None],
                                  modtiles[i], router_w_pad, tm)
        y, w = _moe(h2, logits, router_b, moe_w_gate, moe_w_up, moe_w_down, i)
        if i + 1 < DEPTH:
            xa, h = _resid_norm(xa, y, w, norm_mix[i + 1][None], modtiles[i], modtiles[i + 1])
        else:
            xa = _resid(xa, y, w, modtiles[i])
    return xa.reshape(Bb, TB, D)[:, n_ctx:]
```

```python
import functools
import math

import jax
import jax.numpy as jnp
from jax import lax
from jax.experimental import pallas as pl
from jax.experimental.pallas import tpu as pltpu

D_MODEL = 2048
DEPTH = 4
GRID_W = 64
N_MIXERS = 4
NORM_EPS = 1e-6
ROPE_BASE = 10000.0
CONV_W = 4

LRU_WIDTH = D_MODEL
LRU_BLOCKS = 8
LRU_BLOCK = LRU_WIDTH // LRU_BLOCKS
LRU_C = 8.0

DN_HEADS = 16
DN_DK = D_MODEL // DN_HEADS
DN_DV = D_MODEL // DN_HEADS

RET_HEADS = 8
RET_DK = D_MODEL // RET_HEADS
RET_DV = 2 * D_MODEL // RET_HEADS

ATT_HEADS = 16
ATT_KV_HEADS = 4
ATT_HD = D_MODEL // ATT_HEADS
ATT_G = ATT_HEADS // ATT_KV_HEADS
WINDOW = 128

N_EXPERTS = 32
N_GROUPS = 8
EXPERTS_PER_GROUP = N_EXPERTS // N_GROUPS
TOP_K = 2
D_EXPERT = 512
MOE_BLOCK = 256

F32 = jnp.float32
BF16 = jnp.bfloat16

VMEM_LIMIT_BYTES = 56 * 1024 * 1024
SUBLANES = 8
LANES = 128
ROW_TILE = 256
HALO = SUBLANES


def _params(*sem):
    return pltpu.CompilerParams(dimension_semantics=sem, vmem_limit_bytes=VMEM_LIMIT_BYTES)


def _sigmoid(x):
    return 0.5 * jnp.tanh(0.5 * x) + 0.5


def _silu(x):
    return x * _sigmoid(x)


def _softplus(x):
    return jnp.maximum(x, 0.0) + jnp.log1p(jnp.exp(-jnp.abs(x)))


def _mm_kernel(x_ref, w_ref, o_ref, wb_ref):
    @pl.when(pl.program_id(1) == 0)
    def _():
        wb_ref[...] = w_ref[...].astype(BF16)

    o_ref[...] = jnp.dot(x_ref[...], wb_ref[...], preferred_element_type=F32)


def _matmul(x, w, layer=None, n_out=None, tm=512, tn=1024):
    M, K = x.shape
    N = w.shape[-1] if n_out is None else n_out
    tm = min(tm, M)
    tn = min(tn, N)
    assert M % tm == 0 and N % tn == 0, (M, N, tm, tn)
    if layer is None:
        w_spec = pl.BlockSpec((K, tn), lambda n, m: (0, n))
    else:
        w_spec = pl.BlockSpec((None, K, tn), lambda n, m: (layer, 0, n))
    return pl.pallas_call(
        _mm_kernel,
        grid=(N // tn, M // tm),
        in_specs=[pl.BlockSpec((tm, K), lambda n, m: (m, 0)), w_spec],
        out_specs=pl.BlockSpec((tm, tn), lambda n, m: (m, n)),
        out_shape=jax.ShapeDtypeStruct((M, N), F32),
        scratch_shapes=[pltpu.VMEM((K, tn), BF16)],
        compiler_params=_params("arbitrary", "arbitrary"),
        name="dense_matmul",
    )(x, w)


def _norm_mod(x, normw, shift, scale):
    y = x * lax.rsqrt(jnp.mean(x * x, axis=-1, keepdims=True) + NORM_EPS)
    return (y * normw) * (1.0 + scale) + shift


def _prenorm_kernel(x_ref, nw_ref, mod_ref, h_ref):
    h_ref[...] = _norm_mod(x_ref[...], nw_ref[...], mod_ref[0:1, :], mod_ref[1:2, :]).astype(BF16)


def _prenorm(x, normw, modtile):
    M, D = x.shape
    return pl.pallas_call(
        _prenorm_kernel,
        grid=(M // ROW_TILE,),
        in_specs=[pl.BlockSpec((ROW_TILE, D), lambda t: (t, 0)),
                  pl.BlockSpec((1, D), lambda t: (0, 0)),
                  pl.BlockSpec((None, 8, D), lambda t: (t, 0, 0))],
        out_specs=pl.BlockSpec((ROW_TILE, D), lambda t: (t, 0)),
        out_shape=jax.ShapeDtypeStruct((M, D), BF16),
        compiler_params=_params("arbitrary"),
        name="prenorm",
    )(x, normw, modtile)


def _ffn_residual(x_ref, y_refs, rt_ref, mod_ref):
    f = None
    for k, y_ref in enumerate(y_refs):
        lane = TOP_K + k
        term = y_ref[...] * rt_ref[:, lane:lane + 1]
        f = term if f is None else f + term
    return x_ref[...] + mod_ref[5:6, :] * f


def _resid_norm_kernel(x_ref, y0_ref, y1_ref, rt_ref, nw_ref, mod_ref, modn_ref, xo_ref, h_ref):
    x = _ffn_residual(x_ref, (y0_ref, y1_ref), rt_ref, mod_ref)
    xo_ref[...] = x
    h_ref[...] = _norm_mod(x, nw_ref[...], modn_ref[0:1, :], modn_ref[1:2, :]).astype(BF16)


def _resid_norm(x, ys, route, normw_next, modtile, modtile_next):
    M, D = x.shape
    row = pl.BlockSpec((ROW_TILE, D), lambda t: (t, 0))
    mod = pl.BlockSpec((None, 8, D), lambda t: (t, 0, 0))
    return pl.pallas_call(
        _resid_norm_kernel,
        grid=(M // ROW_TILE,),
        in_specs=[row, row, row, pl.BlockSpec((ROW_TILE, LANES), lambda t: (t, 0)),
                  pl.BlockSpec((1, D), lambda t: (0, 0)), mod, mod],
        out_specs=[row, row],
        out_shape=[jax.ShapeDtypeStruct((M, D), F32), jax.ShapeDtypeStruct((M, D), BF16)],
        compiler_params=_params("arbitrary"),
        name="resid_norm",
    )(x, ys[0], ys[1], route, normw_next, modtile, modtile_next)


def _resid_kernel(x_ref, y0_ref, y1_ref, rt_ref, mod_ref, xo_ref):
    xo_ref[...] = _ffn_residual(x_ref, (y0_ref, y1_ref), rt_ref, mod_ref)


def _resid(x, ys, route, modtile):
    M, D = x.shape
    row = pl.BlockSpec((ROW_TILE, D), lambda t: (t, 0))
    return pl.pallas_call(
        _resid_kernel,
        grid=(M // ROW_TILE,),
        in_specs=[row, row, row, pl.BlockSpec((ROW_TILE, LANES), lambda t: (t, 0)),
                  pl.BlockSpec((None, 8, D), lambda t: (t, 0, 0))],
        out_specs=row,
        out_shape=jax.ShapeDtypeStruct((M, D), F32),
        compiler_params=_params("arbitrary"),
        name="resid",
    )(x, ys[0], ys[1], route, modtile)


def _head_rms(o, width, normw):
    parts = []
    for h in range(o.shape[1] // width):
        oh = o[:, h * width:(h + 1) * width]
        parts.append(oh * lax.rsqrt(jnp.mean(oh * oh, axis=-1, keepdims=True) + NORM_EPS))
    return jnp.concatenate(parts, axis=1) * normw


def _finish_lru(gate_ref, hf_ref, hb_ref):
    return jax.nn.gelu(gate_ref[...]) * (hf_ref[...] + hb_ref[...])


def _finish_dn(z_ref, of_ref, ob_ref, nw_ref):
    return _head_rms(of_ref[...] + ob_ref[...], DN_DV, nw_ref[...]) * _silu(z_ref[...])


def _finish_ret(gate_ref, of_ref, ob_ref, nw_ref):
    return _head_rms(of_ref[...] + ob_ref[...], RET_DV, nw_ref[...]) * _silu(gate_ref[...])


def _finish_att(o_ref):
    return o_ref[...]


ROUTE_WEIGHT_LANE = TOP_K


def _route_tile(logits, bias):
    assert TOP_K == 2 and EXPERTS_PER_GROUP == 4
    lane = lax.broadcasted_iota(jnp.int32, logits.shape, 1)
    valid = lane < N_EXPERTS
    neg = -jnp.inf
    scores = _sigmoid(logits)
    b = jnp.where(valid, scores + bias, neg)
    j = lane % EXPERTS_PER_GROUP
    jf = j.astype(F32)
    gf = (lane // EXPERTS_PER_GROUP).astype(F32)

    def group_reduce(v, op):
        v = op(v, jnp.where(j % 2 == 0, pltpu.roll(v, LANES - 1, axis=1), pltpu.roll(v, 1, axis=1)))
        return op(v, jnp.where(j < 2, pltpu.roll(v, LANES - 2, axis=1), pltpu.roll(v, 2, axis=1)))

    m1 = group_reduce(b, jnp.maximum)
    is1 = jf == group_reduce(jnp.where(b == m1, jf, float(EXPERTS_PER_GROUP)), jnp.minimum)
    b2 = jnp.where(is1, neg, b)
    m2 = group_reduce(b2, jnp.maximum)
    is2 = jf == group_reduce(jnp.where(b2 == m2, jf, float(EXPERTS_PER_GROUP)), jnp.minimum)
    gscore = jnp.where(valid, m1 + m2, neg)
    best = jnp.max(gscore, axis=-1, keepdims=True)
    gsel = jnp.min(jnp.where(gscore == best, gf, float(N_GROUPS)), axis=-1, keepdims=True)
    in_group = valid & (gf == gsel)
    lf = lane.astype(F32)

    def pick(mask, v):
        return jnp.sum(jnp.where(in_group & mask, v, 0.0), axis=-1, keepdims=True)

    e1, e2 = pick(is1, lf), pick(is2, lf)
    s1, s2 = pick(is1, scores), pick(is2, scores)
    tot = s1 + s2
    return jnp.where(lane == 0, e1, jnp.where(lane == 1, e2, jnp.where(
        lane == ROUTE_WEIGHT_LANE, s1 / tot, jnp.where(lane == ROUTE_WEIGHT_LANE + 1, s2 / tot, 0.0))))


def _outproj_kernel(finish, n_fin, *refs):
    fin_refs = refs[:n_fin]
    w_ref, x_ref, nw_ref, mod_ref, rw_ref, rb_ref, xo_ref, h_ref, rt_ref = refs[n_fin:]
    a = finish(*fin_refs).astype(BF16)
    y = jnp.dot(a, w_ref[...], preferred_element_type=F32)
    x = x_ref[...] + mod_ref[2:3, :] * y
    xo_ref[...] = x
    h = _norm_mod(x, nw_ref[...], mod_ref[3:4, :], mod_ref[4:5, :])
    h_ref[...] = h
    logits = jnp.dot(h, rw_ref[...], preferred_element_type=F32, precision=lax.Precision.HIGHEST)
    rt_ref[...] = _route_tile(logits, rb_ref[...])


def _outproj(finish, fin_args, fin_specs, w_bf16, x, normw_ffn, modtile, router_w_pad, router_b_pad, tm):
    M, D = x.shape
    K = w_bf16.shape[0]
    per = ROW_TILE // tm
    row = pl.BlockSpec((tm, D), lambda t: (t, 0))
    in_specs = list(fin_specs) + [
        pl.BlockSpec((K, D), lambda t: (0, 0)),
        row,
        pl.BlockSpec((1, D), lambda t: (0, 0)),
        pl.BlockSpec((None, 8, D), lambda t: (t // per, 0, 0)),
        pl.BlockSpec((D, LANES), lambda t: (0, 0)),
        pl.BlockSpec((1, LANES), lambda t: (0, 0)),
    ]
    return pl.pallas_call(
        functools.partial(_outproj_kernel, finish, len(fin_args)),
        grid=(M // tm,),
        in_specs=in_specs,
        out_specs=[row, row, pl.BlockSpec((tm, LANES), lambda t: (t, 0))],
        out_shape=[jax.ShapeDtypeStruct((M, D), F32), jax.ShapeDtypeStruct((M, D), F32),
                   jax.ShapeDtypeStruct((M, LANES), F32)],
        compiler_params=_params("arbitrary"),
        name="outproj",
    )(*fin_args, w_bf16, x, normw_ffn, modtile, router_w_pad, router_b_pad)


def _rope_tables(n_ctx, seq, head_dim):
    quarter = head_dim // 4
    pos = jnp.arange(seq)
    inv_freq = ROPE_BASE ** (-jnp.arange(quarter, dtype=F32) / quarter)
    ang_r = (pos // GRID_W).astype(F32)[:, None] * inv_freq
    ang_c = (pos % GRID_W).astype(F32)[:, None] * inv_freq
    c = jnp.concatenate([jnp.cos(ang_r), jnp.cos(ang_r), jnp.cos(ang_c), jnp.cos(ang_c)], axis=1)
    s = jnp.concatenate([-jnp.sin(ang_r), jnp.sin(ang_r), -jnp.sin(ang_c), jnp.sin(ang_c)], axis=1)
    c = jnp.concatenate([jnp.ones((n_ctx, head_dim), F32), c], axis=0)
    s = jnp.concatenate([jnp.zeros((n_ctx, head_dim), F32), s], axis=0)
    return c, s


def _rope(x, c, s, quarter):
    width = x.shape[1]
    if 2 * quarter == LANES:
        parts = [pltpu.roll(x[:, j:j + LANES], quarter, axis=1) for j in range(0, width, LANES)]
        partner = parts[0] if len(parts) == 1 else jnp.concatenate(parts, axis=1)
    else:
        assert width == LANES and 4 * quarter == LANES
        lane = lax.broadcasted_iota(jnp.int32, x.shape, 1)
        partner = jnp.where((lane % (2 * quarter)) < quarter,
                            pltpu.roll(x, LANES - quarter, axis=1), pltpu.roll(x, quarter, axis=1))
    return x * c + partner * s


ATT_TILE = 128


def _att_prep_kernel(p_ref, qn_ref, kn_ref, c_ref, s_ref, q_ref, k_ref, v_ref):
    c = c_ref[...]
    s = s_ref[...]
    qw = ATT_HEADS * ATT_HD
    kw = ATT_KV_HEADS * ATT_HD

    def norm_rope(xh, w):
        y = xh * lax.rsqrt(jnp.mean(xh * xh, axis=-1, keepdims=True) + NORM_EPS) * w
        return _rope(y, c, s, ATT_HD // 4)

    for h in range(ATT_HEADS):
        qh = norm_rope(p_ref[:, h * ATT_HD:(h + 1) * ATT_HD], qn_ref[...])
        q_ref[:, h * ATT_HD:(h + 1) * ATT_HD] = (qh * ATT_HD ** -0.5).astype(BF16)
    for h in range(ATT_KV_HEADS):
        kh = norm_rope(p_ref[:, qw + h * ATT_HD:qw + (h + 1) * ATT_HD], kn_ref[...])
        k_ref[:, h * ATT_HD:(h + 1) * ATT_HD] = kh.astype(BF16)
    v_ref[...] = p_ref[:, qw + kw:qw + 2 * kw].astype(BF16)


def _att_prep(p, q_norm, k_norm, rope_c, rope_s, tiles_per_batch):
    M = p.shape[0]
    qw, kw = ATT_HEADS * ATT_HD, ATT_KV_HEADS * ATT_HD
    tab = pl.BlockSpec((ROW_TILE, ATT_HD), lambda t: (t % tiles_per_batch, 0))
    return pl.pallas_call(
        _att_prep_kernel,
        grid=(M // ROW_TILE,),
        in_specs=[pl.BlockSpec((ROW_TILE, qw + 2 * kw), lambda t: (t, 0)),
                  pl.BlockSpec((1, ATT_HD), lambda t: (0, 0)),
                  pl.BlockSpec((1, ATT_HD), lambda t: (0, 0)), tab, tab],
        out_specs=[pl.BlockSpec((ROW_TILE, qw), lambda t: (t, 0)),
                   pl.BlockSpec((ROW_TILE, kw), lambda t: (t, 0)),
                   pl.BlockSpec((ROW_TILE, kw), lambda t: (t, 0))],
        out_shape=[jax.ShapeDtypeStruct((M, qw), BF16), jax.ShapeDtypeStruct((M, kw), BF16),
                   jax.ShapeDtypeStruct((M, kw), BF16)],
        compiler_params=_params("arbitrary"),
        name="att_prep",
    )(p, q_norm, k_norm, rope_c, rope_s)


def _att_kernel(n_ctx_tiles, n_tiles, q_ref, kp_ref, ko_ref, kn_ref, vp_ref, vo_ref, vn_ref,
                kc_ref, vc_ref, sink_ref, o_ref):
    t = pl.program_id(1)
    T = ATT_TILE
    rows = ATT_G * T
    row = lax.broadcasted_iota(jnp.int32, (rows, 3 * T), 0) % T
    col = lax.broadcasted_iota(jnp.int32, (rows, 3 * T), 1)
    blk = col // T
    c = col % T
    latent = t >= n_ctx_tiles
    prev_ok = latent & (t >= n_ctx_tiles + 1)
    next_ok = latent & (t <= n_tiles - 2)
    band = ((blk == 1) & latent) | ((blk == 0) & (c >= row) & prev_ok) | ((blk == 2) & (c <= row) & next_ok)
    nt = (((1,), (1,)), ((), ()))
    KV = range(ATT_KV_HEADS)
    ks = [slice(kvh * ATT_HD, (kvh + 1) * ATT_HD) for kvh in KV]
    q4 = [jnp.concatenate([q_ref[:, (kvh * ATT_G + g) * ATT_HD:(kvh * ATT_G + g + 1) * ATT_HD]
                           for g in range(ATT_G)], axis=0) for kvh in KV]
    s_loc = [lax.dot_general(q4[i], jnp.concatenate([kp_ref[:, ks[i]], ko_ref[:, ks[i]], kn_ref[:, ks[i]]], axis=0),
                             nt, preferred_element_type=F32) for i in KV]
    s_ctx = [lax.dot_general(q4[i], kc_ref[:, ks[i]], nt, preferred_element_type=F32) for i in KV]
    s_loc = [jnp.where(band, s_loc[i], -jnp.inf) for i in KV]
    sink = [sink_ref[i][:, 0:1] for i in KV]
    m = [jnp.maximum(sink[i], jnp.maximum(jnp.max(s_loc[i], axis=-1, keepdims=True),
                                          jnp.max(s_ctx[i], axis=-1, keepdims=True))) for i in KV]
    p_loc = [jnp.exp(s_loc[i] - m[i]) for i in KV]
    p_ctx = [jnp.exp(s_ctx[i] - m[i]) for i in KV]
    den = [jnp.exp(sink[i] - m[i]) + jnp.sum(p_loc[i], axis=-1, keepdims=True)
           + jnp.sum(p_ctx[i], axis=-1, keepdims=True) for i in KV]
    o = [jnp.dot(p_loc[i].astype(BF16),
                 jnp.concatenate([vp_ref[:, ks[i]], vo_ref[:, ks[i]], vn_ref[:, ks[i]]], axis=0),
                 preferred_element_type=F32)
         + jnp.dot(p_ctx[i].astype(BF16), vc_ref[:, ks[i]], preferred_element_type=F32) for i in KV]
    for i in KV:
        oi = o[i] / den[i]
        for g in range(ATT_G):
            h = i * ATT_G + g
            o_ref[:, h * ATT_HD:(h + 1) * ATT_HD] = oi[g * T:(g + 1) * T, :].astype(BF16)


def _attention(q, k, v, sink, n_batch, n_ctx, seq):
    M, qw = q.shape
    kw = k.shape[1]
    T = ATT_TILE
    n_tiles = (n_ctx + seq) // T
    n_ctx_tiles = n_ctx // T
    sink_rows = jnp.broadcast_to(
        jnp.repeat(sink.astype(F32).reshape(ATT_KV_HEADS, ATT_G), T, axis=1)[:, :, None],
        (ATT_KV_HEADS, ATT_G * T, LANES))

    def tile(off):
        return pl.BlockSpec((T, kw), lambda b, t: (b * n_tiles + jnp.clip(t + off, 0, n_tiles - 1), 0))

    ctx = pl.BlockSpec((n_ctx, kw), lambda b, t: (b * (n_tiles // n_ctx_tiles), 0))
    return pl.pallas_call(
        functools.partial(_att_kernel, n_ctx_tiles, n_tiles),
        grid=(n_batch, n_tiles),
        in_specs=[pl.BlockSpec((T, qw), lambda b, t: (b * n_tiles + t, 0)),
                  tile(-1), tile(0), tile(1), tile(-1), tile(0), tile(1), ctx, ctx,
                  pl.BlockSpec((ATT_KV_HEADS, ATT_G * T, LANES), lambda b, t: (0, 0, 0))],
        out_specs=pl.BlockSpec((T, qw), lambda b, t: (b * n_tiles + t, 0)),
        out_shape=jax.ShapeDtypeStruct((M, qw), BF16),
        compiler_params=_params("arbitrary", "arbitrary"),
        name="window_attention",
    )(q, k, k, k, v, v, v, k, v, sink_rows)


def _chunk_of_step(s, n_chunks, n_ctx_chunks, reverse):
    if not reverse:
        return s
    return jnp.where(s < n_ctx_chunks, n_ctx_chunks - 1 - s, n_chunks - 1 - (s - n_ctx_chunks))


RET_CHUNK = ROW_TILE


def _ret_kernel(reverse, q_ref, k_ref, v_ref, c_ref, s_ref, o_ref, state_ref):
    @pl.when(pl.program_id(1) == 0)
    def _():
        state_ref[...] = jnp.zeros_like(state_ref)

    C = RET_CHUNK
    c = c_ref[...]
    s = s_ref[...]
    i = lax.broadcasted_iota(jnp.int32, (C, C), 0)
    j = lax.broadcasted_iota(jnp.int32, (C, C), 1)
    pos = lax.broadcasted_iota(jnp.int32, (C, 1), 0).astype(F32)
    diff = (j - i) if reverse else (i - j)
    mask = diff > 0 if reverse else diff >= 0
    dist = jnp.maximum(diff, 0).astype(F32)
    steps_in = (C - pos) if reverse else (pos + 1.0)
    steps_out = pos if reverse else (C - 1.0 - pos)
    nt = (((1,), (1,)), ((), ()))
    for h in range(RET_HEADS):
        log_gamma = math.log1p(-2.0 ** (-5.0 - h))
        q = _rope(q_ref[:, h * RET_DK:(h + 1) * RET_DK], c, s, RET_DK // 4)
        k = _rope(k_ref[:, h * RET_DK:(h + 1) * RET_DK], c, s, RET_DK // 4) * RET_DK ** -0.5
        v = v_ref[:, h * RET_DV:(h + 1) * RET_DV].astype(BF16)
        qb = q.astype(BF16)
        dmat = jnp.where(mask, jnp.exp(dist * log_gamma), 0.0)
        a = lax.dot_general(qb, k.astype(BF16), nt, preferred_element_type=F32) * dmat
        st = state_ref[h]
        o = jnp.dot(a.astype(BF16), v, preferred_element_type=F32)
        o = o + jnp.dot(qb, st.astype(BF16), preferred_element_type=F32) * jnp.exp(steps_in * log_gamma)
        o_ref[:, h * RET_DV:(h + 1) * RET_DV] = o
        kt = (k * jnp.exp(steps_out * log_gamma)).T.astype(BF16)
        state_ref[h] = st * math.exp(C * log_gamma) + jnp.dot(kt, v, preferred_element_type=F32)


def _retention(p, rope_c, rope_s, n_batch, n_ctx, seq, reverse):
    M = p.shape[0]
    C = RET_CHUNK
    n_chunks = (n_ctx + seq) // C
    n_ctx_chunks = n_ctx // C
    hk, hv = RET_HEADS * RET_DK, RET_HEADS * RET_DV

    def chunk(b, s):
        return _chunk_of_step(s, n_chunks, n_ctx_chunks, reverse)

    tab = pl.BlockSpec((C, RET_DK), lambda b, s: (chunk(b, s), 0))
    return pl.pallas_call(
        functools.partial(_ret_kernel, reverse),
        grid=(n_batch, n_chunks),
        in_specs=[pl.BlockSpec((C, hk), lambda b, s: (b * n_chunks + chunk(b, s), 0)),
                  pl.BlockSpec((C, hk), lambda b, s: (b * n_chunks + chunk(b, s), 1)),
                  pl.BlockSpec((C, hv), lambda b, s: (b * n_chunks + chunk(b, s), 1)),
                  tab, tab],
        out_specs=pl.BlockSpec((C, hv), lambda b, s: (b * n_chunks + chunk(b, s), 0)),
        out_shape=jax.ShapeDtypeStruct((M, hv), F32),
        scratch_shapes=[pltpu.VMEM((RET_HEADS, RET_DK, RET_DV), F32)],
        compiler_params=_params("arbitrary", "arbitrary"),
        name="retention_rev" if reverse else "retention_fwd",
    )(p, p, p, rope_c, rope_s)


LRU_CHUNK = ROW_TILE


def _lru_kernel(reverse, n_chunks, n_ctx_chunks, x_ref, xp_ref, xn_ref, cw_ref, cb_ref, wa_ref, ba_ref,
                wx_ref, bx_ref, lam_ref, o_ref, h_ref):
    step = pl.program_id(1)

    @pl.when(step == 0)
    def _():
        h_ref[...] = jnp.zeros_like(h_ref)

    C = LRU_CHUNK
    W = LRU_BLOCK
    chunk = _chunk_of_step(step, n_chunks, n_ctx_chunks, reverse)
    has_prev = ((chunk != 0) & (chunk != n_ctx_chunks)).astype(F32)
    has_next = ((chunk != n_ctx_chunks - 1) & (chunk != n_chunks - 1)).astype(F32)
    rows = C + 2 * HALO
    sub = lax.broadcasted_iota(jnp.int32, (C, W), 0) % SUBLANES

    def block(n, carry):
        lanes = pl.ds(pl.multiple_of(n * W, W), W)
        full = jnp.concatenate([xp_ref[:, lanes] * has_prev, x_ref[:, lanes], xn_ref[:, lanes] * has_next], axis=0)
        cw = cw_ref[:, lanes]
        xc = (cw[0:1] * pltpu.roll(full, 2, axis=0)[HALO:HALO + C]
              + cw[1:2] * pltpu.roll(full, 1, axis=0)[HALO:HALO + C]
              + cw[2:3] * full[HALO:HALO + C]
              + cw[3:4] * pltpu.roll(full, rows - 1, axis=0)[HALO:HALO + C]) + cb_ref[:, lanes]
        xb = xc.astype(BF16)
        r = _sigmoid(jnp.dot(xb, wa_ref[n], preferred_element_type=F32) + ba_ref[:, lanes])
        gi = _sigmoid(jnp.dot(xb, wx_ref[n], preferred_element_type=F32) + bx_ref[:, lanes])
        a = jnp.exp((-LRU_C) * r * _softplus(-lam_ref[:, lanes]))
        b = jnp.sqrt(jnp.maximum(1.0 - a * a, 0.0)) * (gi * xc)
        for k in (1, 2, 4):
            if reverse:
                keep = sub < SUBLANES - k
                a_sh = pltpu.roll(a, C - k, axis=0)
                b_sh = pltpu.roll(b, C - k, axis=0)
            else:
                keep = sub >= k
                a_sh = pltpu.roll(a, k, axis=0)
                b_sh = pltpu.roll(b, k, axis=0)
            b = b + a * jnp.where(keep, b_sh, 0.0)
            a = a * jnp.where(keep, a_sh, 1.0)
        hin = h_ref[0:1, lanes]
        groups = C // SUBLANES
        outs = [None] * groups
        for g in (range(groups - 1, -1, -1) if reverse else range(groups)):
            lo = g * SUBLANES
            hg = b[lo:lo + SUBLANES] + a[lo:lo + SUBLANES] * hin
            outs[g] = hg
            hin = hg[0:1] if reverse else hg[SUBLANES - 1:SUBLANES]
        o_ref[:, lanes] = jnp.concatenate(outs, axis=0)
        h_ref[:, lanes] = jnp.broadcast_to(hin, (SUBLANES, W))
        return carry

    lax.fori_loop(0, LRU_BLOCKS, block, 0)


def _rglru(p, conv_w, conv_b, gate_a_w, gate_a_b, gate_x_w, gate_x_b, lam, n_batch, n_ctx, seq, reverse):
    M = p.shape[0]
    C = LRU_CHUNK
    Wd = LRU_WIDTH
    n_chunks = (n_ctx + seq) // C
    n_ctx_chunks = n_ctx // C
    per = C // HALO
    last = M // HALO - 1

    def tile(b, s):
        return b * n_chunks + _chunk_of_step(s, n_chunks, n_ctx_chunks, reverse)

    vec = pl.BlockSpec((1, Wd), lambda b, s: (0, 0))
    wts = pl.BlockSpec((LRU_BLOCKS, LRU_BLOCK, LRU_BLOCK), lambda b, s: (0, 0, 0))
    return pl.pallas_call(
        functools.partial(_lru_kernel, reverse, n_chunks, n_ctx_chunks),
        grid=(n_batch, n_chunks),
        in_specs=[pl.BlockSpec((C, Wd), lambda b, s: (tile(b, s), 1)),
                  pl.BlockSpec((HALO, Wd), lambda b, s: (jnp.maximum(tile(b, s) * per - 1, 0), 1)),
                  pl.BlockSpec((HALO, Wd), lambda b, s: (jnp.minimum((tile(b, s) + 1) * per, last), 1)),
                  pl.BlockSpec((SUBLANES, Wd), lambda b, s: (0, 0)),
                  vec, wts, vec, wts, vec, vec],
        out_specs=pl.BlockSpec((C, Wd), lambda b, s: (tile(b, s), 0)),
        out_shape=jax.ShapeDtypeStruct((M, Wd), F32),
        scratch_shapes=[pltpu.VMEM((SUBLANES, Wd), F32)],
        compiler_params=_params("arbitrary", "arbitrary"),
        name="rglru_rev" if reverse else "rglru_fwd",
    )(p, p, p, conv_w, conv_b, gate_a_w, gate_a_b, gate_x_w, gate_x_b, lam)


DN_CHUNK = 128
DN_HEAD_GROUP = 8
DN_INV_BASE = SUBLANES
DN_NEUMANN_ROUNDS = 2


def _dn_kernel(reverse, d, n_chunks, n_ctx_chunks, q_ref, k_ref, v_ref, xp_ref, xn_ref, tail_ref, cw_ref,
               alog_ref, dtb_ref, o_ref, state_ref, gct_ref, bt_ref):
    step = pl.program_id(1)

    @pl.when(step == 0)
    def _():
        state_ref[...] = jnp.zeros_like(state_ref)

    C = DN_CHUNK
    H = DN_HEADS
    HK = H * DN_DK
    chunk = _chunk_of_step(step, n_chunks, n_ctx_chunks, reverse)
    has_prev = ((chunk != 0) & (chunk != n_ctx_chunks)).astype(F32)
    has_next = ((chunk != n_ctx_chunks - 1) & (chunk != n_chunks - 1)).astype(F32)
    rows = C + 2 * HALO
    ci = lax.broadcasted_iota(jnp.int32, (C, C), 0)
    si = lax.broadcasted_iota(jnp.int32, (C, C), 1)
    mask = (si >= ci) if reverse else (si <= ci)
    strict = (si > ci) if reverse else (si < ci)
    eye = (si == ci).astype(F32)
    last = 0 if reverse else C - 1
    same_base = (ci // DN_INV_BASE) == (si // DN_INV_BASE)
    merge_masks = []
    size = DN_INV_BASE
    while size < C:
        cb, sb = ci // size, si // size
        merge_masks.append(((cb % 2 == 0) & (sb == cb + 1)) if reverse else ((cb % 2 == 1) & (sb == cb - 1)))
        size *= 2

    tail = tail_ref[...]
    beta = _sigmoid(tail)
    g = -jnp.exp(alog_ref[...]) * _softplus(tail + dtb_ref[...])
    gcum = jnp.dot(mask.astype(F32), g, preferred_element_type=F32, precision=lax.Precision.HIGHEST)
    gct_ref[...] = gcum.T
    bt_ref[...] = beta.T
    nt = (((1,), (1,)), ((), ()))

    def conv_silu(ref, ref_off, h):
        lanes_in = pl.ds(pl.multiple_of(h * DN_DK, DN_DK), DN_DK)
        lanes_all = pl.ds(pl.multiple_of(ref_off + h * DN_DK, DN_DK), DN_DK)
        full = jnp.concatenate([xp_ref[:, lanes_all] * has_prev, ref[:, lanes_in], xn_ref[:, lanes_all] * has_next],
                               axis=0)
        cw = cw_ref[:, lanes_all]
        y = (cw[0:1] * pltpu.roll(full, 2, axis=0)[HALO:HALO + C]
             + cw[1:2] * pltpu.roll(full, 1, axis=0)[HALO:HALO + C]
             + cw[2:3] * full[HALO:HALO + C]
             + cw[3:4] * pltpu.roll(full, rows - 1, axis=0)[HALO:HALO + C])
        return _silu(y)

    def l2n(x):
        return x * lax.rsqrt(jnp.sum(x * x, axis=-1, keepdims=True) + NORM_EPS)

    def dot(a, b):
        return jnp.dot(a.astype(BF16), b.astype(BF16), preferred_element_type=F32)

    def dot_nt(a, b):
        return lax.dot_general(a.astype(BF16), b.astype(BF16), nt, preferred_element_type=F32)

    def group(gi, carry):
        hs = [gi * DN_HEAD_GROUP + jj for jj in range(DN_HEAD_GROUP)]
        G = range(DN_HEAD_GROUP)
        sts = [state_ref[h] for h in hs]
        g_row = [gct_ref[pl.ds(2 * H + d * H + h, 1), :] for h in hs]
        g_rows = [jnp.broadcast_to(g_row[i], (C, C)) for i in G]
        g_cols = [g_rows[i].T for i in G]
        b_cols = [jnp.broadcast_to(bt_ref[pl.ds(d * H + h, 1), :], (C, C)).T for h in hs]
        g_last = [g_row[i][:, last:last + 1] for i in G]
        decay = [jnp.where(mask, jnp.exp(jnp.where(mask, g_cols[i] - g_rows[i], 0.0)), 0.0) for i in G]
        q = [l2n(conv_silu(q_ref, 0, h)) * DN_DK ** -0.5 for h in hs]
        k = [l2n(conv_silu(k_ref, HK, h)) for h in hs]
        v = [conv_silu(v_ref, 2 * HK, h) for h in hs]
        kb = [k[i] * b_cols[i] for i in G]
        m = [jnp.where(strict, dot_nt(kb[i], k[i]) * decay[i], 0.0) for i in G]
        a_intra = [jnp.where(mask, dot_nt(q[i], k[i]) * decay[i], 0.0) for i in G]

        n = [jnp.where(same_base, -m[i], 0.0) for i in G]
        t = [eye + n[i] for i in G]
        for _ in range(DN_NEUMANN_ROUNDS):
            n = [dot(n[i], n[i]) for i in G]
            t = [t[i] + dot(t[i], n[i]) for i in G]
        for pair in merge_masks:
            x = [dot(t[i], jnp.where(pair, m[i], 0.0)) for i in G]
            t = [t[i] - dot(x[i], t[i]) for i in G]
        sol = [dot(t[i], jnp.concatenate([v[i] * b_cols[i], kb[i] * jnp.exp(g_cols[i])], axis=1)) for i in G]

        r = [dot(jnp.concatenate([sol[i][:, DN_DV:], q[i] * jnp.exp(g_cols[i])], axis=0), sts[i]) for i in G]
        v_new = [sol[i][:, :DN_DV] - r[i][:C] for i in G]
        o = [r[i][C:] + dot(a_intra[i], v_new[i]) for i in G]
        st_new = [sts[i] * jnp.exp(g_last[i]) + dot((k[i] * jnp.exp(g_last[i] - g_cols[i])).T, v_new[i]) for i in G]
        for i, h in enumerate(hs):
            o_ref[:, pl.ds(pl.multiple_of(h * DN_DV, DN_DV), DN_DV)] = o[i]
            state_ref[h] = st_new[i]
        return carry

    lax.fori_loop(0, H // DN_HEAD_GROUP, group, 0)


def _deltanet(p, tail, conv_w, alog_row, dtb_row, n_batch, n_ctx, seq, d):
    M = p.shape[0]
    C = DN_CHUNK
    reverse = d == 1
    n_chunks = (n_ctx + seq) // C
    n_ctx_chunks = n_ctx // C
    HK = DN_HEADS * DN_DK
    per = C // HALO
    last = M // HALO - 1

    def tile(b, s):
        return b * n_chunks + _chunk_of_step(s, n_chunks, n_ctx_chunks, reverse)

    row1 = pl.BlockSpec((1, LANES), lambda b, s: (0, 0))
    return pl.pallas_call(
        functools.partial(_dn_kernel, reverse, d, n_chunks, n_ctx_chunks),
        grid=(n_batch, n_chunks),
        in_specs=[pl.BlockSpec((C, HK), lambda b, s: (tile(b, s), 0)),
                  pl.BlockSpec((C, HK), lambda b, s: (tile(b, s), 1)),
                  pl.BlockSpec((C, HK), lambda b, s: (tile(b, s), 2)),
                  pl.BlockSpec((HALO, 4 * HK), lambda b, s: (jnp.maximum(tile(b, s) * per - 1, 0), 0)),
                  pl.BlockSpec((HALO, 4 * HK), lambda b, s: (jnp.minimum((tile(b, s) + 1) * per, last), 0)),
                  pl.BlockSpec((C, LANES), lambda b, s: (tile(b, s), 0)),
                  pl.BlockSpec((SUBLANES, 3 * HK), lambda b, s: (0, 0)),
                  row1, row1],
        out_specs=pl.BlockSpec((C, HK), lambda b, s: (tile(b, s), 0)),
        out_shape=jax.ShapeDtypeStruct((M, HK), F32),
        scratch_shapes=[pltpu.VMEM((DN_HEADS, DN_DK, DN_DV), F32),
                        pltpu.VMEM((LANES, C), F32), pltpu.VMEM((LANES, C), F32)],
        compiler_params=_params("arbitrary", "arbitrary"),
        name="deltanet_rev" if reverse else "deltanet_fwd",
    )(p, p, p, p, p, tail, conv_w, alog_row, dtb_row)


def _moe_kernel(be_ref, nu_ref, xs_ref, wg_ref, wu_ref, wd_ref, o_ref, wgb, wub, wdb):
    b = pl.program_id(0)
    e = be_ref[b]
    prev = be_ref[jnp.maximum(b - 1, 0)]
    used = b < nu_ref[0]

    @pl.when(used & ((b == 0) | (e != prev)))
    def _():
        wgb[...] = wg_ref[...].astype(BF16)
        wub[...] = wu_ref[...].astype(BF16)
        wdb[...] = wd_ref[...].astype(BF16)

    @pl.when(used)
    def _():
        x = xs_ref[...].astype(BF16)
        g = jnp.dot(x, wgb[...], preferred_element_type=F32)
        u = jnp.dot(x, wub[...], preferred_element_type=F32)
        a = (_silu(g) * u).astype(BF16)
        o_ref[...] = jnp.dot(a, wdb[...], preferred_element_type=F32)

    @pl.when(jnp.logical_not(used))
    def _():
        o_ref[...] = jnp.zeros_like(o_ref)


def _moe_experts(xs, block_expert, n_used, w_gate, w_up, w_down, layer):
    n_slots, D = xs.shape
    n_blocks = n_slots // MOE_BLOCK
    DE = w_gate.shape[-1]
    grid_spec = pltpu.PrefetchScalarGridSpec(
        num_scalar_prefetch=2,
        grid=(n_blocks,),
        in_specs=[pl.BlockSpec((MOE_BLOCK, D), lambda b, be, nu: (b, 0)),
                  pl.BlockSpec((None, None, D, DE), lambda b, be, nu: (layer, be[b], 0, 0)),
                  pl.BlockSpec((None, None, D, DE), lambda b, be, nu: (layer, be[b], 0, 0)),
                  pl.BlockSpec((None, None, DE, D), lambda b, be, nu: (layer, be[b], 0, 0))],
        out_specs=pl.BlockSpec((MOE_BLOCK, D), lambda b, be, nu: (b, 0)),
        scratch_shapes=[pltpu.VMEM((D, DE), BF16), pltpu.VMEM((D, DE), BF16), pltpu.VMEM((DE, D), BF16)],
    )
    return pl.pallas_call(
        _moe_kernel,
        grid_spec=grid_spec,
        out_shape=jax.ShapeDtypeStruct((n_slots, D), F32),
        compiler_params=_params("arbitrary"),
        name="moe_experts",
    )(block_expert.astype(jnp.int32), n_used.astype(jnp.int32).reshape(1), xs, w_gate, w_up, w_down)


RANK_BLOCK = 128


def _moe(h2, route, w_gate, w_up, w_down, layer):
    N = h2.shape[0]
    NK = N * TOP_K
    assert NK % RANK_BLOCK == 0
    flat_e = route[:, :TOP_K].astype(jnp.int32).reshape(NK)
    onehot = flat_e[:, None] == jnp.arange(N_EXPERTS, dtype=jnp.int32)[None, :]
    oh = onehot.astype(BF16).reshape(NK // RANK_BLOCK, RANK_BLOCK, N_EXPERTS)
    tri = jnp.tril(jnp.ones((RANK_BLOCK, RANK_BLOCK), BF16))
    within = jnp.einsum('ij,bjk->bik', tri, oh, preferred_element_type=F32)
    tot = within[:, -1, :]
    before = jnp.cumsum(tot, axis=0) - tot
    ranks = (within + before[:, None, :]).reshape(NK, N_EXPERTS)
    rank = jnp.sum(jnp.where(onehot, ranks, 0.0), axis=1).astype(jnp.int32) - 1
    counts = (before[-1] + tot[-1]).astype(jnp.int32)
    padded = (counts + MOE_BLOCK - 1) // MOE_BLOCK * MOE_BLOCK
    pad_end = jnp.cumsum(padded)
    pad_start = pad_end - padded
    dest = pad_start[flat_e] + rank
    n_blocks = (NK + N_EXPERTS * (MOE_BLOCK - 1)) // MOE_BLOCK
    n_slots = n_blocks * MOE_BLOCK
    slot_tok = jnp.zeros((n_slots,), jnp.int32).at[dest].set(jnp.arange(NK, dtype=jnp.int32) // TOP_K)
    block_start = jnp.arange(n_blocks, dtype=jnp.int32) * MOE_BLOCK
    block_expert = jnp.minimum(jnp.sum((pad_end[None, :] <= block_start[:, None]).astype(jnp.int32), axis=1),
                               N_EXPERTS - 1)
    xs = h2[slot_tok]
    ys = _moe_experts(xs, block_expert, pad_end[-1] // MOE_BLOCK, w_gate, w_up, w_down, layer)
    dest2 = dest.reshape(N, TOP_K)
    return [ys[dest2[:, k]] for k in range(TOP_K)]


def _tile_mod(mod_rows, n_batch, tiles_per_batch, n_ctx_tiles):
    D = mod_rows.shape[1] // 6
    t = jnp.arange(n_batch * tiles_per_batch)
    src = jnp.where(t % tiles_per_batch < n_ctx_tiles, n_batch, t // tiles_per_batch)
    table = mod_rows.reshape(mod_rows.shape[0], 6, D)[src]
    return jnp.pad(table, ((0, 0), (0, 2), (0, 0)))


def kernel(x, c, ctx, c_ctx, mod_w, mod_b, norm_mix, norm_ffn, lru_w_in, lru_conv_w, lru_conv_b, lru_gate_a_w, lru_gate_a_b, lru_gate_x_w, lru_gate_x_b, lru_lambda, lru_w_out, dn_w_in, dn_conv_w, dn_a_log, dn_dt_bias, dn_norm, dn_w_out, ret_w_in, ret_norm, ret_w_out, att_w_in, att_q_norm, att_k_norm, att_sink, att_w_out, router_w, router_b, moe_w_gate, moe_w_up, moe_w_down):
    Bb, S, D = x.shape
    n_ctx = ctx.shape[1]
    TB = n_ctx + S
    M = Bb * TB
    tiles_per_batch = TB // ROW_TILE
    assert n_ctx % ROW_TILE == 0 and S % ROW_TILE == 0 and Bb < 16

    xa = jnp.concatenate([ctx, x], axis=1).reshape(M, D)
    cond = jnp.zeros((16, D), F32).at[:Bb].set(jax.nn.silu(c)).at[Bb].set(jax.nn.silu(c_ctx)).astype(BF16)
    modtiles = [_tile_mod(_matmul(cond, mod_w, layer=i, tn=D) + mod_b[i], Bb, tiles_per_batch, n_ctx // ROW_TILE)
                for i in range(DEPTH)]
    router_w_pad = jnp.pad(router_w.astype(F32), ((0, 0), (0, LANES - N_EXPERTS)))
    router_b_pad = jnp.pad(router_b.astype(F32), (0, LANES - N_EXPERTS))[None]
    rope_att = _rope_tables(n_ctx, S, ATT_HD)
    rope_ret = _rope_tables(n_ctx, S, RET_DK)

    def row(width, col=0, dtype_rows=ROW_TILE):
        return pl.BlockSpec((dtype_rows, width), lambda t: (t, col))

    def vec(width):
        return pl.BlockSpec((1, width), lambda t: (0, 0))

    h = _prenorm(xa, norm_mix[0][None], modtiles[0])
    for i in range(DEPTH):
        kind, j = i % N_MIXERS, i // N_MIXERS
        tm = ROW_TILE
        if kind == 0:
            p = _matmul(h, lru_w_in, layer=j)
            cw = jnp.pad(lru_conv_w[j], ((0, SUBLANES - CONV_W), (0, 0)))
            dirs = [_rglru(p, cw, lru_conv_b[j][None], lru_gate_a_w[j, d].astype(BF16), lru_gate_a_b[j, d][None],
                           lru_gate_x_w[j, d].astype(BF16), lru_gate_x_b[j, d][None], lru_lambda[j, d][None],
                           Bb, n_ctx, S, reverse=(d == 1)) for d in range(2)]
            fin = (_finish_lru, [p] + dirs, [row(LRU_WIDTH, 0), row(LRU_WIDTH), row(LRU_WIDTH)])
            w_out = lru_w_out[j]
        elif kind == 1:
            HK = DN_HEADS * DN_DK
            p = _matmul(h, dn_w_in, layer=j, n_out=4 * HK)
            tail = _matmul(h, jnp.pad(dn_w_in[j][:, 4 * HK:], ((0, 0), (0, LANES - 4 * DN_HEADS))))
            cw = jnp.pad(dn_conv_w[j], ((0, SUBLANES - CONV_W), (0, 0)))
            alog_row = jnp.zeros((1, LANES), F32).at[0, 2 * DN_HEADS:4 * DN_HEADS].set(dn_a_log[j].reshape(-1))
            dtb_row = jnp.zeros((1, LANES), F32).at[0, 2 * DN_HEADS:4 * DN_HEADS].set(dn_dt_bias[j].reshape(-1))
            dirs = [_deltanet(p, tail, cw, alog_row, dtb_row, Bb, n_ctx, S, d) for d in range(2)]
            fin = (_finish_dn, [p] + dirs + [jnp.tile(dn_norm[j], DN_HEADS)[None]],
                   [row(HK, 3), row(HK), row(HK), vec(HK)])
            w_out = dn_w_out[j]
        elif kind == 2:
            tm = ROW_TILE // 2
            HV = RET_HEADS * RET_DV
            p = _matmul(h, ret_w_in, layer=j)
            dirs = [_retention(p, rope_ret[0], rope_ret[1], Bb, n_ctx, S, reverse=(d == 1)) for d in range(2)]
            fin = (_finish_ret, [p] + dirs + [ret_norm[j][None]],
                   [row(HV, 2, tm), row(HV, 0, tm), row(HV, 0, tm), vec(HV)])
            w_out = ret_w_out[j]
        else:
            p = _matmul(h, att_w_in, layer=j)
            q, k, v = _att_prep(p, att_q_norm[j][None], att_k_norm[j][None], rope_att[0], rope_att[1],
                                tiles_per_batch)
            o = _attention(q, k, v, att_sink[j], Bb, n_ctx, S)
            fin = (_finish_att, [o], [row(ATT_HEADS * ATT_HD)])
            w_out = att_w_out[j]

        xa, h2, route = _outproj(fin[0], fin[1], fin[2], w_out.astype(BF16), xa, norm_ffn[i][None],
                                 modtiles[i], router_w_pad, router_b_pad, tm)
        ys = _moe(h2, route, moe_w_gate, moe_w_up, moe_w_down, i)
        if i + 1 < DEPTH:
            xa, h = _resid_norm(xa, ys, route, norm_mix[i + 1][None], modtiles[i], modtiles[i + 1])
        else:
            xa = _resid(xa, ys, route, modtiles[i])
    return xa.reshape(Bb, TB, D)[:, n_ctx:]
```

```python
import functools
import math

import jax
import jax.numpy as jnp
from jax import lax
from jax.experimental import pallas as pl
from jax.experimental.pallas import tpu as pltpu

D_MODEL = 2048
DEPTH = 4
GRID_W = 64
N_MIXERS = 4
NORM_EPS = 1e-6
ROPE_BASE = 10000.0
CONV_W = 4

LRU_WIDTH = D_MODEL
LRU_BLOCKS = 8
LRU_BLOCK = LRU_WIDTH // LRU_BLOCKS
LRU_C = 8.0

DN_HEADS = 16
DN_DK = D_MODEL // DN_HEADS
DN_DV = D_MODEL // DN_HEADS

RET_HEADS = 8
RET_DK = D_MODEL // RET_HEADS
RET_DV = 2 * D_MODEL // RET_HEADS

ATT_HEADS = 16
ATT_KV_HEADS = 4
ATT_HD = D_MODEL // ATT_HEADS
ATT_G = ATT_HEADS // ATT_KV_HEADS
WINDOW = 128

N_EXPERTS = 32
N_GROUPS = 8
EXPERTS_PER_GROUP = N_EXPERTS // N_GROUPS
TOP_K = 2
D_EXPERT = 512
MOE_BLOCK = 256

F32 = jnp.float32
BF16 = jnp.bfloat16

VMEM_LIMIT_BYTES = 56 * 1024 * 1024
SUBLANES = 8
LANES = 128
ROW_TILE = 256
HALO = SUBLANES


def _params(*sem):
    return pltpu.CompilerParams(dimension_semantics=sem, vmem_limit_bytes=VMEM_LIMIT_BYTES)


def _sigmoid(x):
    return 0.5 * jnp.tanh(0.5 * x) + 0.5


def _silu(x):
    return x * _sigmoid(x)


def _softplus(x):
    return jnp.maximum(x, 0.0) + jnp.log1p(jnp.exp(-jnp.abs(x)))


def _mm_kernel(x_ref, w_ref, o_ref, wb_ref):
    @pl.when(pl.program_id(1) == 0)
    def _():
        wb_ref[...] = w_ref[...].astype(BF16)

    o_ref[...] = jnp.dot(x_ref[...], wb_ref[...], preferred_element_type=F32)


def _matmul(x, w, layer=None, n_out=None, tm=512, tn=1024):
    M, K = x.shape
    N = w.shape[-1] if n_out is None else n_out
    tm = min(tm, M)
    tn = min(tn, N)
    assert M % tm == 0 and N % tn == 0, (M, N, tm, tn)
    if layer is None:
        w_spec = pl.BlockSpec((K, tn), lambda n, m: (0, n))
    else:
        w_spec = pl.BlockSpec((None, K, tn), lambda n, m: (layer, 0, n))
    return pl.pallas_call(
        _mm_kernel,
        grid=(N // tn, M // tm),
        in_specs=[pl.BlockSpec((tm, K), lambda n, m: (m, 0)), w_spec],
        out_specs=pl.BlockSpec((tm, tn), lambda n, m: (m, n)),
        out_shape=jax.ShapeDtypeStruct((M, N), F32),
        scratch_shapes=[pltpu.VMEM((K, tn), BF16)],
        compiler_params=_params("arbitrary", "arbitrary"),
        name="dense_matmul",
    )(x, w)


def _mod_kernel(c_ref, w_ref, b_ref, o_ref):
    o_ref[...] = jnp.dot(c_ref[...], w_ref[...].astype(BF16), preferred_element_type=F32) + b_ref[...]


def _adaln_rows(cond, mod_w, mod_b, tn=2048):
    R, D = cond.shape
    L, _, N = mod_w.shape
    return pl.pallas_call(
        _mod_kernel,
        grid=(L, N // tn),
        in_specs=[pl.BlockSpec((R, D), lambda l, n: (0, 0)),
                  pl.BlockSpec((None, D, tn), lambda l, n: (l, 0, n)),
                  pl.BlockSpec((None, 1, tn), lambda l, n: (l, 0, n))],
        out_specs=pl.BlockSpec((None, R, tn), lambda l, n: (l, 0, n)),
        out_shape=jax.ShapeDtypeStruct((L, R, N), F32),
        compiler_params=_params("arbitrary", "arbitrary"),
        name="adaln_rows",
    )(cond, mod_w, mod_b.reshape(L, 1, N))


def _norm_mod(x, normw, shift, scale):
    y = x * lax.rsqrt(jnp.mean(x * x, axis=-1, keepdims=True) + NORM_EPS)
    return (y * normw) * (1.0 + scale) + shift


def _prenorm_kernel(x_ref, nw_ref, mod_ref, h_ref):
    h_ref[...] = _norm_mod(x_ref[...], nw_ref[...], mod_ref[0:1, :], mod_ref[1:2, :]).astype(BF16)


def _prenorm(x, normw, modtile):
    M, D = x.shape
    return pl.pallas_call(
        _prenorm_kernel,
        grid=(M // ROW_TILE,),
        in_specs=[pl.BlockSpec((ROW_TILE, D), lambda t: (t, 0)),
                  pl.BlockSpec((1, D), lambda t: (0, 0)),
                  pl.BlockSpec((None, 8, D), lambda t: (t, 0, 0))],
        out_specs=pl.BlockSpec((ROW_TILE, D), lambda t: (t, 0)),
        out_shape=jax.ShapeDtypeStruct((M, D), BF16),
        compiler_params=_params("arbitrary"),
        name="prenorm",
    )(x, normw, modtile)


def _ffn_residual(x_ref, y_refs, rt_ref, mod_ref):
    f = None
    for k, y_ref in enumerate(y_refs):
        lane = TOP_K + k
        term = y_ref[...] * rt_ref[:, lane:lane + 1]
        f = term if f is None else f + term
    return x_ref[...] + mod_ref[5:6, :] * f


def _resid_norm_kernel(x_ref, y0_ref, y1_ref, rt_ref, nw_ref, mod_ref, modn_ref, xo_ref, h_ref):
    x = _ffn_residual(x_ref, (y0_ref, y1_ref), rt_ref, mod_ref)
    xo_ref[...] = x
    h_ref[...] = _norm_mod(x, nw_ref[...], modn_ref[0:1, :], modn_ref[1:2, :]).astype(BF16)


def _resid_norm(x, ys, route, normw_next, modtile, modtile_next):
    M, D = x.shape
    row = pl.BlockSpec((ROW_TILE, D), lambda t: (t, 0))
    mod = pl.BlockSpec((None, 8, D), lambda t: (t, 0, 0))
    return pl.pallas_call(
        _resid_norm_kernel,
        grid=(M // ROW_TILE,),
        in_specs=[row, row, row, pl.BlockSpec((ROW_TILE, LANES), lambda t: (t, 0)),
                  pl.BlockSpec((1, D), lambda t: (0, 0)), mod, mod],
        out_specs=[row, row],
        out_shape=[jax.ShapeDtypeStruct((M, D), F32), jax.ShapeDtypeStruct((M, D), BF16)],
        compiler_params=_params("arbitrary"),
        name="resid_norm",
    )(x, ys[0], ys[1], route, normw_next, modtile, modtile_next)


def _resid_kernel(x_ref, y0_ref, y1_ref, rt_ref, mod_ref, xo_ref):
    xo_ref[...] = _ffn_residual(x_ref, (y0_ref, y1_ref), rt_ref, mod_ref)


def _resid(x, ys, route, modtile):
    M, D = x.shape
    row = pl.BlockSpec((ROW_TILE, D), lambda t: (t, 0))
    return pl.pallas_call(
        _resid_kernel,
        grid=(M // ROW_TILE,),
        in_specs=[row, row, row, pl.BlockSpec((ROW_TILE, LANES), lambda t: (t, 0)),
                  pl.BlockSpec((None, 8, D), lambda t: (t, 0, 0))],
        out_specs=row,
        out_shape=jax.ShapeDtypeStruct((M, D), F32),
        compiler_params=_params("arbitrary"),
        name="resid",
    )(x, ys[0], ys[1], route, modtile)


def _head_rms(o, width, normw):
    parts = []
    for h in range(o.shape[1] // width):
        oh = o[:, h * width:(h + 1) * width]
        parts.append(oh * lax.rsqrt(jnp.mean(oh * oh, axis=-1, keepdims=True) + NORM_EPS))
    return jnp.concatenate(parts, axis=1) * normw


def _finish_lru(gate_ref, hf_ref, hb_ref):
    return jax.nn.gelu(gate_ref[...]) * (hf_ref[...] + hb_ref[...])


def _finish_dn(z_ref, of_ref, ob_ref, nw_ref):
    return _head_rms(of_ref[...] + ob_ref[...], DN_DV, nw_ref[...]) * _silu(z_ref[...])


def _finish_ret(gate_ref, of_ref, ob_ref, nw_ref):
    return _head_rms(of_ref[...] + ob_ref[...], RET_DV, nw_ref[...]) * _silu(gate_ref[...])


def _finish_att(o_ref):
    return o_ref[...]


ROUTE_WEIGHT_LANE = TOP_K


def _route_tile(logits, bias):
    assert TOP_K == 2 and EXPERTS_PER_GROUP == 4
    lane = lax.broadcasted_iota(jnp.int32, logits.shape, 1)
    valid = lane < N_EXPERTS
    neg = -jnp.inf
    scores = _sigmoid(logits)
    b = jnp.where(valid, scores + bias, neg)
    j = lane % EXPERTS_PER_GROUP
    jf = j.astype(F32)
    gf = (lane // EXPERTS_PER_GROUP).astype(F32)

    def group_reduce(v, op):
        v = op(v, jnp.where(j % 2 == 0, pltpu.roll(v, LANES - 1, axis=1), pltpu.roll(v, 1, axis=1)))
        return op(v, jnp.where(j < 2, pltpu.roll(v, LANES - 2, axis=1), pltpu.roll(v, 2, axis=1)))

    m1 = group_reduce(b, jnp.maximum)
    is1 = jf == group_reduce(jnp.where(b == m1, jf, float(EXPERTS_PER_GROUP)), jnp.minimum)
    b2 = jnp.where(is1, neg, b)
    m2 = group_reduce(b2, jnp.maximum)
    is2 = jf == group_reduce(jnp.where(b2 == m2, jf, float(EXPERTS_PER_GROUP)), jnp.minimum)
    gscore = jnp.where(valid, m1 + m2, neg)
    best = jnp.max(gscore, axis=-1, keepdims=True)
    gsel = jnp.min(jnp.where(gscore == best, gf, float(N_GROUPS)), axis=-1, keepdims=True)
    in_group = valid & (gf == gsel)
    lf = lane.astype(F32)

    def pick(mask, v):
        return jnp.sum(jnp.where(in_group & mask, v, 0.0), axis=-1, keepdims=True)

    e1, e2 = pick(is1, lf), pick(is2, lf)
    s1, s2 = pick(is1, scores), pick(is2, scores)
    tot = s1 + s2
    return jnp.where(lane == 0, e1, jnp.where(lane == 1, e2, jnp.where(
        lane == ROUTE_WEIGHT_LANE, s1 / tot, jnp.where(lane == ROUTE_WEIGHT_LANE + 1, s2 / tot, 0.0))))


def _outproj_kernel(finish, n_fin, *refs):
    fin_refs = refs[:n_fin]
    w_ref, x_ref, nw_ref, mod_ref, rw_ref, rb_ref, xo_ref, h_ref, rt_ref = refs[n_fin:]
    a = finish(*fin_refs).astype(BF16)
    y = jnp.dot(a, w_ref[...], preferred_element_type=F32)
    x = x_ref[...] + mod_ref[2:3, :] * y
    xo_ref[...] = x
    h = _norm_mod(x, nw_ref[...], mod_ref[3:4, :], mod_ref[4:5, :])
    h_ref[...] = h
    h_hi = h.astype(BF16)
    h_lo = (h - h_hi.astype(F32)).astype(BF16)
    logits = (jnp.dot(h_hi, rw_ref[0], preferred_element_type=F32)
              + jnp.dot(h_lo, rw_ref[0], preferred_element_type=F32)
              + jnp.dot(h_hi, rw_ref[1], preferred_element_type=F32))
    rt_ref[...] = _route_tile(logits, rb_ref[...])


def _outproj(finish, fin_args, fin_specs, w_bf16, x, normw_ffn, modtile, router_w_pad, router_b_pad, tm):
    M, D = x.shape
    K = w_bf16.shape[0]
    per = ROW_TILE // tm
    row = pl.BlockSpec((tm, D), lambda t: (t, 0))
    in_specs = list(fin_specs) + [
        pl.BlockSpec((K, D), lambda t: (0, 0)),
        row,
        pl.BlockSpec((1, D), lambda t: (0, 0)),
        pl.BlockSpec((None, 8, D), lambda t: (t // per, 0, 0)),
        pl.BlockSpec((2, D, LANES), lambda t: (0, 0, 0)),
        pl.BlockSpec((1, LANES), lambda t: (0, 0)),
    ]
    return pl.pallas_call(
        functools.partial(_outproj_kernel, finish, len(fin_args)),
        grid=(M // tm,),
        in_specs=in_specs,
        out_specs=[row, row, pl.BlockSpec((tm, LANES), lambda t: (t, 0))],
        out_shape=[jax.ShapeDtypeStruct((M, D), F32), jax.ShapeDtypeStruct((M, D), F32),
                   jax.ShapeDtypeStruct((M, LANES), F32)],
        compiler_params=_params("arbitrary"),
        name="outproj",
    )(*fin_args, w_bf16, x, normw_ffn, modtile, router_w_pad, router_b_pad)


def _rope_tables(n_ctx, seq, head_dim):
    quarter = head_dim // 4
    pos = jnp.arange(seq)
    inv_freq = ROPE_BASE ** (-jnp.arange(quarter, dtype=F32) / quarter)
    ang_r = (pos // GRID_W).astype(F32)[:, None] * inv_freq
    ang_c = (pos % GRID_W).astype(F32)[:, None] * inv_freq
    c = jnp.concatenate([jnp.cos(ang_r), jnp.cos(ang_r), jnp.cos(ang_c), jnp.cos(ang_c)], axis=1)
    s = jnp.concatenate([-jnp.sin(ang_r), jnp.sin(ang_r), -jnp.sin(ang_c), jnp.sin(ang_c)], axis=1)
    c = jnp.concatenate([jnp.ones((n_ctx, head_dim), F32), c], axis=0)
    s = jnp.concatenate([jnp.zeros((n_ctx, head_dim), F32), s], axis=0)
    return c, s


def _rope(x, c, s, quarter):
    width = x.shape[1]
    if 2 * quarter == LANES:
        parts = [pltpu.roll(x[:, j:j + LANES], quarter, axis=1) for j in range(0, width, LANES)]
        partner = parts[0] if len(parts) == 1 else jnp.concatenate(parts, axis=1)
    else:
        assert width == LANES and 4 * quarter == LANES
        lane = lax.broadcasted_iota(jnp.int32, x.shape, 1)
        partner = jnp.where((lane % (2 * quarter)) < quarter,
                            pltpu.roll(x, LANES - quarter, axis=1), pltpu.roll(x, quarter, axis=1))
    return x * c + partner * s


ATT_TILE = 128


def _att_prep_kernel(p_ref, qn_ref, kn_ref, c_ref, s_ref, q_ref, k_ref, v_ref):
    c = c_ref[...]
    s = s_ref[...]
    qw = ATT_HEADS * ATT_HD
    kw = ATT_KV_HEADS * ATT_HD

    def norm_rope(xh, w):
        y = xh * lax.rsqrt(jnp.mean(xh * xh, axis=-1, keepdims=True) + NORM_EPS) * w
        return _rope(y, c, s, ATT_HD // 4)

    for h in range(ATT_HEADS):
        qh = norm_rope(p_ref[:, h * ATT_HD:(h + 1) * ATT_HD], qn_ref[...])
        q_ref[:, h * ATT_HD:(h + 1) * ATT_HD] = (qh * ATT_HD ** -0.5).astype(BF16)
    for h in range(ATT_KV_HEADS):
        kh = norm_rope(p_ref[:, qw + h * ATT_HD:qw + (h + 1) * ATT_HD], kn_ref[...])
        k_ref[:, h * ATT_HD:(h + 1) * ATT_HD] = kh.astype(BF16)
    v_ref[...] = p_ref[:, qw + kw:qw + 2 * kw].astype(BF16)


def _att_prep(p, q_norm, k_norm, rope_c, rope_s, tiles_per_batch):
    M = p.shape[0]
    qw, kw = ATT_HEADS * ATT_HD, ATT_KV_HEADS * ATT_HD
    tab = pl.BlockSpec((ROW_TILE, ATT_HD), lambda t: (t % tiles_per_batch, 0))
    return pl.pallas_call(
        _att_prep_kernel,
        grid=(M // ROW_TILE,),
        in_specs=[pl.BlockSpec((ROW_TILE, qw + 2 * kw), lambda t: (t, 0)),
                  pl.BlockSpec((1, ATT_HD), lambda t: (0, 0)),
                  pl.BlockSpec((1, ATT_HD), lambda t: (0, 0)), tab, tab],
        out_specs=[pl.BlockSpec((ROW_TILE, qw), lambda t: (t, 0)),
                   pl.BlockSpec((ROW_TILE, kw), lambda t: (t, 0)),
                   pl.BlockSpec((ROW_TILE, kw), lambda t: (t, 0))],
        out_shape=[jax.ShapeDtypeStruct((M, qw), BF16), jax.ShapeDtypeStruct((M, kw), BF16),
                   jax.ShapeDtypeStruct((M, kw), BF16)],
        compiler_params=_params("arbitrary"),
        name="att_prep",
    )(p, q_norm, k_norm, rope_c, rope_s)


def _att_kernel(n_ctx_tiles, n_tiles, q_ref, kp_ref, ko_ref, kn_ref, vp_ref, vo_ref, vn_ref,
                kc_ref, vc_ref, sink_ref, o_ref):
    t = pl.program_id(1)
    T = ATT_TILE
    rows = ATT_G * T
    row = lax.broadcasted_iota(jnp.int32, (rows, 3 * T), 0) % T
    col = lax.broadcasted_iota(jnp.int32, (rows, 3 * T), 1)
    blk = col // T
    c = col % T
    latent = t >= n_ctx_tiles
    prev_ok = latent & (t >= n_ctx_tiles + 1)
    next_ok = latent & (t <= n_tiles - 2)
    band = ((blk == 1) & latent) | ((blk == 0) & (c >= row) & prev_ok) | ((blk == 2) & (c <= row) & next_ok)
    nt = (((1,), (1,)), ((), ()))
    KV = range(ATT_KV_HEADS)
    ks = [slice(kvh * ATT_HD, (kvh + 1) * ATT_HD) for kvh in KV]
    q4 = [jnp.concatenate([q_ref[:, (kvh * ATT_G + g) * ATT_HD:(kvh * ATT_G + g + 1) * ATT_HD]
                           for g in range(ATT_G)], axis=0) for kvh in KV]
    s_loc = [lax.dot_general(q4[i], jnp.concatenate([kp_ref[:, ks[i]], ko_ref[:, ks[i]], kn_ref[:, ks[i]]], axis=0),
                             nt, preferred_element_type=F32) for i in KV]
    s_ctx = [lax.dot_general(q4[i], kc_ref[:, ks[i]], nt, preferred_element_type=F32) for i in KV]
    s_loc = [jnp.where(band, s_loc[i], -jnp.inf) for i in KV]
    sink = [sink_ref[i][:, 0:1] for i in KV]
    m = [jnp.maximum(sink[i], jnp.maximum(jnp.max(s_loc[i], axis=-1, keepdims=True),
                                          jnp.max(s_ctx[i], axis=-1, keepdims=True))) for i in KV]
    p_loc = [jnp.exp(s_loc[i] - m[i]) for i in KV]
    p_ctx = [jnp.exp(s_ctx[i] - m[i]) for i in KV]
    den = [jnp.exp(sink[i] - m[i]) + jnp.sum(p_loc[i], axis=-1, keepdims=True)
           + jnp.sum(p_ctx[i], axis=-1, keepdims=True) for i in KV]
    o = [jnp.dot(p_loc[i].astype(BF16),
                 jnp.concatenate([vp_ref[:, ks[i]], vo_ref[:, ks[i]], vn_ref[:, ks[i]]], axis=0),
                 preferred_element_type=F32)
         + jnp.dot(p_ctx[i].astype(BF16), vc_ref[:, ks[i]], preferred_element_type=F32) for i in KV]
    for i in KV:
        oi = o[i] / den[i]
        for g in range(ATT_G):
            h = i * ATT_G + g
            o_ref[:, h * ATT_HD:(h + 1) * ATT_HD] = oi[g * T:(g + 1) * T, :].astype(BF16)


def _attention(q, k, v, sink, n_batch, n_ctx, seq):
    M, qw = q.shape
    kw = k.shape[1]
    T = ATT_TILE
    n_tiles = (n_ctx + seq) // T
    n_ctx_tiles = n_ctx // T
    sink_rows = jnp.broadcast_to(
        jnp.repeat(sink.astype(F32).reshape(ATT_KV_HEADS, ATT_G), T, axis=1)[:, :, None],
        (ATT_KV_HEADS, ATT_G * T, LANES))

    def tile(off):
        return pl.BlockSpec((T, kw), lambda b, t: (b * n_tiles + jnp.clip(t + off, 0, n_tiles - 1), 0))

    ctx = pl.BlockSpec((n_ctx, kw), lambda b, t: (b * (n_tiles // n_ctx_tiles), 0))
    return pl.pallas_call(
        functools.partial(_att_kernel, n_ctx_tiles, n_tiles),
        grid=(n_batch, n_tiles),
        in_specs=[pl.BlockSpec((T, qw), lambda b, t: (b * n_tiles + t, 0)),
                  tile(-1), tile(0), tile(1), tile(-1), tile(0), tile(1), ctx, ctx,
                  pl.BlockSpec((ATT_KV_HEADS, ATT_G * T, LANES), lambda b, t: (0, 0, 0))],
        out_specs=pl.BlockSpec((T, qw), lambda b, t: (b * n_tiles + t, 0)),
        out_shape=jax.ShapeDtypeStruct((M, qw), BF16),
        compiler_params=_params("arbitrary", "arbitrary"),
        name="window_attention",
    )(q, k, k, k, v, v, v, k, v, sink_rows)


def _chunk_of_step(s, n_chunks, n_ctx_chunks, reverse):
    if not reverse:
        return s
    return jnp.where(s < n_ctx_chunks, n_ctx_chunks - 1 - s, n_chunks - 1 - (s - n_ctx_chunks))


RET_CHUNK = ROW_TILE


def _ret_kernel(reverse, q_ref, k_ref, v_ref, c_ref, s_ref, o_ref, state_ref):
    @pl.when(pl.program_id(1) == 0)
    def _():
        state_ref[...] = jnp.zeros_like(state_ref)

    C = RET_CHUNK
    c = c_ref[...]
    s = s_ref[...]
    i = lax.broadcasted_iota(jnp.int32, (C, C), 0)
    j = lax.broadcasted_iota(jnp.int32, (C, C), 1)
    pos = lax.broadcasted_iota(jnp.int32, (C, 1), 0).astype(F32)
    diff = (j - i) if reverse else (i - j)
    mask = diff > 0 if reverse else diff >= 0
    dist = jnp.maximum(diff, 0).astype(F32)
    steps_in = (C - pos) if reverse else (pos + 1.0)
    steps_out = pos if reverse else (C - 1.0 - pos)
    nt = (((1,), (1,)), ((), ()))
    for h in range(RET_HEADS):
        log_gamma = math.log1p(-2.0 ** (-5.0 - h))
        q = _rope(q_ref[:, h * RET_DK:(h + 1) * RET_DK], c, s, RET_DK // 4)
        k = _rope(k_ref[:, h * RET_DK:(h + 1) * RET_DK], c, s, RET_DK // 4) * RET_DK ** -0.5
        v = v_ref[:, h * RET_DV:(h + 1) * RET_DV].astype(BF16)
        qb = q.astype(BF16)
        dmat = jnp.where(mask, jnp.exp(dist * log_gamma), 0.0)
        a = lax.dot_general(qb, k.astype(BF16), nt, preferred_element_type=F32) * dmat
        st = state_ref[h]
        o = jnp.dot(a.astype(BF16), v, preferred_element_type=F32)
        o = o + jnp.dot(qb, st.astype(BF16), preferred_element_type=F32) * jnp.exp(steps_in * log_gamma)
        o_ref[:, h * RET_DV:(h + 1) * RET_DV] = o
        kt = (k * jnp.exp(steps_out * log_gamma)).T.astype(BF16)
        state_ref[h] = st * math.exp(C * log_gamma) + jnp.dot(kt, v, preferred_element_type=F32)


def _retention(p, rope_c, rope_s, n_batch, n_ctx, seq, reverse):
    M = p.shape[0]
    C = RET_CHUNK
    n_chunks = (n_ctx + seq) // C
    n_ctx_chunks = n_ctx // C
    hk, hv = RET_HEADS * RET_DK, RET_HEADS * RET_DV

    def chunk(b, s):
        return _chunk_of_step(s, n_chunks, n_ctx_chunks, reverse)

    tab = pl.BlockSpec((C, RET_DK), lambda b, s: (chunk(b, s), 0))
    return pl.pallas_call(
        functools.partial(_ret_kernel, reverse),
        grid=(n_batch, n_chunks),
        in_specs=[pl.BlockSpec((C, hk), lambda b, s: (b * n_chunks + chunk(b, s), 0)),
                  pl.BlockSpec((C, hk), lambda b, s: (b * n_chunks + chunk(b, s), 1)),
                  pl.BlockSpec((C, hv), lambda b, s: (b * n_chunks + chunk(b, s), 1)),
                  tab, tab],
        out_specs=pl.BlockSpec((C, hv), lambda b, s: (b * n_chunks + chunk(b, s), 0)),
        out_shape=jax.ShapeDtypeStruct((M, hv), F32),
        scratch_shapes=[pltpu.VMEM((RET_HEADS, RET_DK, RET_DV), F32)],
        compiler_params=_params("arbitrary", "arbitrary"),
        name="retention_rev" if reverse else "retention_fwd",
    )(p, p, p, rope_c, rope_s)


LRU_CHUNK = ROW_TILE


def _lru_kernel(reverse, n_chunks, n_ctx_chunks, x_ref, xp_ref, xn_ref, cw_ref, cb_ref, wa_ref, ba_ref,
                wx_ref, bx_ref, lam_ref, o_ref, h_ref):
    step = pl.program_id(1)

    @pl.when(step == 0)
    def _():
        h_ref[...] = jnp.zeros_like(h_ref)

    C = LRU_CHUNK
    W = LRU_BLOCK
    chunk = _chunk_of_step(step, n_chunks, n_ctx_chunks, reverse)
    has_prev = ((chunk != 0) & (chunk != n_ctx_chunks)).astype(F32)
    has_next = ((chunk != n_ctx_chunks - 1) & (chunk != n_chunks - 1)).astype(F32)
    rows = C + 2 * HALO
    sub = lax.broadcasted_iota(jnp.int32, (C, W), 0) % SUBLANES

    def block(n, carry):
        lanes = pl.ds(pl.multiple_of(n * W, W), W)
        full = jnp.concatenate([xp_ref[:, lanes] * has_prev, x_ref[:, lanes], xn_ref[:, lanes] * has_next], axis=0)
        cw = cw_ref[:, lanes]
        xc = (cw[0:1] * pltpu.roll(full, 2, axis=0)[HALO:HALO + C]
              + cw[1:2] * pltpu.roll(full, 1, axis=0)[HALO:HALO + C]
              + cw[2:3] * full[HALO:HALO + C]
              + cw[3:4] * pltpu.roll(full, rows - 1, axis=0)[HALO:HALO + C]) + cb_ref[:, lanes]
        xb = xc.astype(BF16)
        r = _sigmoid(jnp.dot(xb, wa_ref[n], preferred_element_type=F32) + ba_ref[:, lanes])
        gi = _sigmoid(jnp.dot(xb, wx_ref[n], preferred_element_type=F32) + bx_ref[:, lanes])
        a = jnp.exp((-LRU_C) * r * _softplus(-lam_ref[:, lanes]))
        b = jnp.sqrt(jnp.maximum(1.0 - a * a, 0.0)) * (gi * xc)
        for k in (1, 2, 4):
            if reverse:
                keep = sub < SUBLANES - k
                a_sh = pltpu.roll(a, C - k, axis=0)
                b_sh = pltpu.roll(b, C - k, axis=0)
            else:
                keep = sub >= k
                a_sh = pltpu.roll(a, k, axis=0)
                b_sh = pltpu.roll(b, k, axis=0)
            b = b + a * jnp.where(keep, b_sh, 0.0)
            a = a * jnp.where(keep, a_sh, 1.0)
        hin = h_ref[0:1, lanes]
        groups = C // SUBLANES
        outs = [None] * groups
        for g in (range(groups - 1, -1, -1) if reverse else range(groups)):
            lo = g * SUBLANES
            hg = b[lo:lo + SUBLANES] + a[lo:lo + SUBLANES] * hin
            outs[g] = hg
            hin = hg[0:1] if reverse else hg[SUBLANES - 1:SUBLANES]
        o_ref[:, lanes] = jnp.concatenate(outs, axis=0)
        h_ref[:, lanes] = jnp.broadcast_to(hin, (SUBLANES, W))
        return carry

    lax.fori_loop(0, LRU_BLOCKS, block, 0)


def _rglru(p, conv_w, conv_b, gate_a_w, gate_a_b, gate_x_w, gate_x_b, lam, n_batch, n_ctx, seq, reverse):
    M = p.shape[0]
    C = LRU_CHUNK
    Wd = LRU_WIDTH
    n_chunks = (n_ctx + seq) // C
    n_ctx_chunks = n_ctx // C
    per = C // HALO
    last = M // HALO - 1

    def tile(b, s):
        return b * n_chunks + _chunk_of_step(s, n_chunks, n_ctx_chunks, reverse)

    vec = pl.BlockSpec((1, Wd), lambda b, s: (0, 0))
    wts = pl.BlockSpec((LRU_BLOCKS, LRU_BLOCK, LRU_BLOCK), lambda b, s: (0, 0, 0))
    return pl.pallas_call(
        functools.partial(_lru_kernel, reverse, n_chunks, n_ctx_chunks),
        grid=(n_batch, n_chunks),
        in_specs=[pl.BlockSpec((C, Wd), lambda b, s: (tile(b, s), 1)),
                  pl.BlockSpec((HALO, Wd), lambda b, s: (jnp.maximum(tile(b, s) * per - 1, 0), 1)),
                  pl.BlockSpec((HALO, Wd), lambda b, s: (jnp.minimum((tile(b, s) + 1) * per, last), 1)),
                  pl.BlockSpec((SUBLANES, Wd), lambda b, s: (0, 0)),
                  vec, wts, vec, wts, vec, vec],
        out_specs=pl.BlockSpec((C, Wd), lambda b, s: (tile(b, s), 0)),
        out_shape=jax.ShapeDtypeStruct((M, Wd), F32),
        scratch_shapes=[pltpu.VMEM((SUBLANES, Wd), F32)],
        compiler_params=_params("arbitrary", "arbitrary"),
        name="rglru_rev" if reverse else "rglru_fwd",
    )(p, p, p, conv_w, conv_b, gate_a_w, gate_a_b, gate_x_w, gate_x_b, lam)


DN_CHUNK = 128
DN_HEAD_GROUP = 8
DN_INV_BASE = SUBLANES
DN_NEUMANN_ROUNDS = 2


def _dn_kernel(reverse, d, n_chunks, n_ctx_chunks, q_ref, k_ref, v_ref, xp_ref, xn_ref, tail_ref, cw_ref,
               alog_ref, dtb_ref, o_ref, state_ref, gct_ref, bt_ref):
    step = pl.program_id(1)

    @pl.when(step == 0)
    def _():
        state_ref[...] = jnp.zeros_like(state_ref)

    C = DN_CHUNK
    H = DN_HEADS
    HK = H * DN_DK
    chunk = _chunk_of_step(step, n_chunks, n_ctx_chunks, reverse)
    has_prev = ((chunk != 0) & (chunk != n_ctx_chunks)).astype(F32)
    has_next = ((chunk != n_ctx_chunks - 1) & (chunk != n_chunks - 1)).astype(F32)
    rows = C + 2 * HALO
    ci = lax.broadcasted_iota(jnp.int32, (C, C), 0)
    si = lax.broadcasted_iota(jnp.int32, (C, C), 1)
    mask = (si >= ci) if reverse else (si <= ci)
    strict = (si > ci) if reverse else (si < ci)
    eye = (si == ci).astype(F32)
    last = 0 if reverse else C - 1
    same_base = (ci // DN_INV_BASE) == (si // DN_INV_BASE)
    merge_masks = []
    size = DN_INV_BASE
    while size < C:
        cb, sb = ci // size, si // size
        merge_masks.append(((cb % 2 == 0) & (sb == cb + 1)) if reverse else ((cb % 2 == 1) & (sb == cb - 1)))
        size *= 2

    tail = tail_ref[...]
    beta = _sigmoid(tail)
    g = -jnp.exp(alog_ref[...]) * _softplus(tail + dtb_ref[...])
    gcum = jnp.dot(mask.astype(F32), g, preferred_element_type=F32, precision=lax.Precision.HIGHEST)
    gct_ref[...] = gcum.T
    bt_ref[...] = beta.T
    nt = (((1,), (1,)), ((), ()))

    def conv_silu(ref, ref_off, h):
        lanes_in = pl.ds(pl.multiple_of(h * DN_DK, DN_DK), DN_DK)
        lanes_all = pl.ds(pl.multiple_of(ref_off + h * DN_DK, DN_DK), DN_DK)
        full = jnp.concatenate([xp_ref[:, lanes_all] * has_prev, ref[:, lanes_in], xn_ref[:, lanes_all] * has_next],
                               axis=0)
        cw = cw_ref[:, lanes_all]
        y = (cw[0:1] * pltpu.roll(full, 2, axis=0)[HALO:HALO + C]
             + cw[1:2] * pltpu.roll(full, 1, axis=0)[HALO:HALO + C]
             + cw[2:3] * full[HALO:HALO + C]
             + cw[3:4] * pltpu.roll(full, rows - 1, axis=0)[HALO:HALO + C])
        return _silu(y)

    def l2n(x):
        return x * lax.rsqrt(jnp.sum(x * x, axis=-1, keepdims=True) + NORM_EPS)

    def dot(a, b):
        return jnp.dot(a.astype(BF16), b.astype(BF16), preferred_element_type=F32)

    def dot_nt(a, b):
        return lax.dot_general(a.astype(BF16), b.astype(BF16), nt, preferred_element_type=F32)

    def group(gi, carry):
        hs = [gi * DN_HEAD_GROUP + jj for jj in range(DN_HEAD_GROUP)]
        G = range(DN_HEAD_GROUP)
        sts = [state_ref[h] for h in hs]
        g_row = [gct_ref[pl.ds(2 * H + d * H + h, 1), :] for h in hs]
        g_rows = [jnp.broadcast_to(g_row[i], (C, C)) for i in G]
        g_cols = [g_rows[i].T for i in G]
        b_cols = [jnp.broadcast_to(bt_ref[pl.ds(d * H + h, 1), :], (C, C)).T for h in hs]
        g_last = [g_row[i][:, last:last + 1] for i in G]
        decay = [jnp.where(mask, jnp.exp(jnp.where(mask, g_cols[i] - g_rows[i], 0.0)), 0.0) for i in G]
        q = [l2n(conv_silu(q_ref, 0, h)) * DN_DK ** -0.5 for h in hs]
        k = [l2n(conv_silu(k_ref, HK, h)) for h in hs]
        v = [conv_silu(v_ref, 2 * HK, h) for h in hs]
        kb = [k[i] * b_cols[i] for i in G]
        m = [jnp.where(strict, dot_nt(kb[i], k[i]) * decay[i], 0.0) for i in G]
        a_intra = [jnp.where(mask, dot_nt(q[i], k[i]) * decay[i], 0.0) for i in G]

        n = [jnp.where(same_base, -m[i], 0.0) for i in G]
        t = [eye + n[i] for i in G]
        for _ in range(DN_NEUMANN_ROUNDS):
            n = [dot(n[i], n[i]) for i in G]
            t = [t[i] + dot(t[i], n[i]) for i in G]
        for pair in merge_masks:
            x = [dot(t[i], jnp.where(pair, m[i], 0.0)) for i in G]
            t = [t[i] - dot(x[i], t[i]) for i in G]
        sol = [dot(t[i], jnp.concatenate([v[i] * b_cols[i], kb[i] * jnp.exp(g_cols[i])], axis=1)) for i in G]

        r = [dot(jnp.concatenate([sol[i][:, DN_DV:], q[i] * jnp.exp(g_cols[i])], axis=0), sts[i]) for i in G]
        v_new = [sol[i][:, :DN_DV] - r[i][:C] for i in G]
        o = [r[i][C:] + dot(a_intra[i], v_new[i]) for i in G]
        st_new = [sts[i] * jnp.exp(g_last[i]) + dot((k[i] * jnp.exp(g_last[i] - g_cols[i])).T, v_new[i]) for i in G]
        for i, h in enumerate(hs):
            o_ref[:, pl.ds(pl.multiple_of(h * DN_DV, DN_DV), DN_DV)] = o[i]
            state_ref[h] = st_new[i]
        return carry

    lax.fori_loop(0, H // DN_HEAD_GROUP, group, 0)


def _deltanet(p, tail, conv_w, alog_row, dtb_row, n_batch, n_ctx, seq, d):
    M = p.shape[0]
    C = DN_CHUNK
    reverse = d == 1
    n_chunks = (n_ctx + seq) // C
    n_ctx_chunks = n_ctx // C
    HK = DN_HEADS * DN_DK
    per = C // HALO
    last = M // HALO - 1

    def tile(b, s):
        return b * n_chunks + _chunk_of_step(s, n_chunks, n_ctx_chunks, reverse)

    row1 = pl.BlockSpec((1, LANES), lambda b, s: (0, 0))
    return pl.pallas_call(
        functools.partial(_dn_kernel, reverse, d, n_chunks, n_ctx_chunks),
        grid=(n_batch, n_chunks),
        in_specs=[pl.BlockSpec((C, HK), lambda b, s: (tile(b, s), 0)),
                  pl.BlockSpec((C, HK), lambda b, s: (tile(b, s), 1)),
                  pl.BlockSpec((C, HK), lambda b, s: (tile(b, s), 2)),
                  pl.BlockSpec((HALO, 4 * HK), lambda b, s: (jnp.maximum(tile(b, s) * per - 1, 0), 0)),
                  pl.BlockSpec((HALO, 4 * HK), lambda b, s: (jnp.minimum((tile(b, s) + 1) * per, last), 0)),
                  pl.BlockSpec((C, LANES), lambda b, s: (tile(b, s), 0)),
                  pl.BlockSpec((SUBLANES, 3 * HK), lambda b, s: (0, 0)),
                  row1, row1],
        out_specs=pl.BlockSpec((C, HK), lambda b, s: (tile(b, s), 0)),
        out_shape=jax.ShapeDtypeStruct((M, HK), F32),
        scratch_shapes=[pltpu.VMEM((DN_HEADS, DN_DK, DN_DV), F32),
                        pltpu.VMEM((LANES, C), F32), pltpu.VMEM((LANES, C), F32)],
        compiler_params=_params("arbitrary", "arbitrary"),
        name="deltanet_rev" if reverse else "deltanet_fwd",
    )(p, p, p, p, p, tail, conv_w, alog_row, dtb_row)


def _moe_kernel(be_ref, nu_ref, xs_ref, wg_ref, wu_ref, wd_ref, o_ref, wgb, wub, wdb):
    b = pl.program_id(0)
    e = be_ref[b]
    prev = be_ref[jnp.maximum(b - 1, 0)]
    used = b < nu_ref[0]

    @pl.when(used & ((b == 0) | (e != prev)))
    def _():
        wgb[...] = wg_ref[...].astype(BF16)
        wub[...] = wu_ref[...].astype(BF16)
        wdb[...] = wd_ref[...].astype(BF16)

    @pl.when(used)
    def _():
        x = xs_ref[...].astype(BF16)
        g = jnp.dot(x, wgb[...], preferred_element_type=F32)
        u = jnp.dot(x, wub[...], preferred_element_type=F32)
        a = (_silu(g) * u).astype(BF16)
        o_ref[...] = jnp.dot(a, wdb[...], preferred_element_type=F32)

    @pl.when(jnp.logical_not(used))
    def _():
        o_ref[...] = jnp.zeros_like(o_ref)


def _moe_experts(xs, block_expert, n_used, w_gate, w_up, w_down, layer):
    n_slots, D = xs.shape
    n_blocks = n_slots // MOE_BLOCK
    DE = w_gate.shape[-1]
    grid_spec = pltpu.PrefetchScalarGridSpec(
        num_scalar_prefetch=2,
        grid=(n_blocks,),
        in_specs=[pl.BlockSpec((MOE_BLOCK, D), lambda b, be, nu: (b, 0)),
                  pl.BlockSpec((None, None, D, DE), lambda b, be, nu: (layer, be[b], 0, 0)),
                  pl.BlockSpec((None, None, D, DE), lambda b, be, nu: (layer, be[b], 0, 0)),
                  pl.BlockSpec((None, None, DE, D), lambda b, be, nu: (layer, be[b], 0, 0))],
        out_specs=pl.BlockSpec((MOE_BLOCK, D), lambda b, be, nu: (b, 0)),
        scratch_shapes=[pltpu.VMEM((D, DE), BF16), pltpu.VMEM((D, DE), BF16), pltpu.VMEM((DE, D), BF16)],
    )
    return pl.pallas_call(
        _moe_kernel,
        grid_spec=grid_spec,
        out_shape=jax.ShapeDtypeStruct((n_slots, D), F32),
        compiler_params=_params("arbitrary"),
        name="moe_experts",
    )(block_expert.astype(jnp.int32), n_used.astype(jnp.int32).reshape(1), xs, w_gate, w_up, w_down)


RANK_BLOCK = 128


def _moe(h2, route, w_gate, w_up, w_down, layer):
    N = h2.shape[0]
    NK = N * TOP_K
    assert NK % RANK_BLOCK == 0
    flat_e = route[:, :TOP_K].astype(jnp.int32).reshape(NK)
    onehot = flat_e[:, None] == jnp.arange(N_EXPERTS, dtype=jnp.int32)[None, :]
    oh = onehot.astype(BF16).reshape(NK // RANK_BLOCK, RANK_BLOCK, N_EXPERTS)
    tri = jnp.tril(jnp.ones((RANK_BLOCK, RANK_BLOCK), BF16))
    within = jnp.einsum('ij,bjk->bik', tri, oh, preferred_element_type=F32)
    tot = within[:, -1, :]
    before = jnp.cumsum(tot, axis=0) - tot
    ranks = (within + before[:, None, :]).reshape(NK, N_EXPERTS)
    rank = jnp.sum(jnp.where(onehot, ranks, 0.0), axis=1).astype(jnp.int32) - 1
    counts = (before[-1] + tot[-1]).astype(jnp.int32)
    padded = (counts + MOE_BLOCK - 1) // MOE_BLOCK * MOE_BLOCK
    pad_end = jnp.cumsum(padded)
    pad_start = pad_end - padded
    dest = pad_start[flat_e] + rank
    n_blocks = (NK + N_EXPERTS * (MOE_BLOCK - 1)) // MOE_BLOCK
    n_slots = n_blocks * MOE_BLOCK
    slot_tok = (jnp.arange(n_slots, dtype=jnp.int32) % N).at[dest].set(jnp.arange(NK, dtype=jnp.int32) // TOP_K)
    block_start = jnp.arange(n_blocks, dtype=jnp.int32) * MOE_BLOCK
    block_expert = jnp.minimum(jnp.sum((pad_end[None, :] <= block_start[:, None]).astype(jnp.int32), axis=1),
                               N_EXPERTS - 1)
    xs = h2[slot_tok]
    ys = _moe_experts(xs, block_expert, pad_end[-1] // MOE_BLOCK, w_gate, w_up, w_down, layer)
    dest2 = dest.reshape(N, TOP_K)
    return [ys[dest2[:, k]] for k in range(TOP_K)]


def _tile_mod(mod_rows, n_batch, tiles_per_batch, n_ctx_tiles):
    D = mod_rows.shape[1] // 6
    t = jnp.arange(n_batch * tiles_per_batch)
    src = jnp.where(t % tiles_per_batch < n_ctx_tiles, n_batch, t // tiles_per_batch)
    table = mod_rows.reshape(mod_rows.shape[0], 6, D)[src]
    return jnp.pad(table, ((0, 0), (0, 2), (0, 0)))


def kernel(x, c, ctx, c_ctx, mod_w, mod_b, norm_mix, norm_ffn, lru_w_in, lru_conv_w, lru_conv_b, lru_gate_a_w, lru_gate_a_b, lru_gate_x_w, lru_gate_x_b, lru_lambda, lru_w_out, dn_w_in, dn_conv_w, dn_a_log, dn_dt_bias, dn_norm, dn_w_out, ret_w_in, ret_norm, ret_w_out, att_w_in, att_q_norm, att_k_norm, att_sink, att_w_out, router_w, router_b, moe_w_gate, moe_w_up, moe_w_down):
    Bb, S, D = x.shape
    n_ctx = ctx.shape[1]
    TB = n_ctx + S
    M = Bb * TB
    tiles_per_batch = TB // ROW_TILE
    assert n_ctx % ROW_TILE == 0 and S % ROW_TILE == 0 and Bb < 16

    xa = jnp.concatenate([ctx, x], axis=1).reshape(M, D)
    cond = jnp.zeros((16, D), F32).at[:Bb].set(jax.nn.silu(c)).at[Bb].set(jax.nn.silu(c_ctx)).astype(BF16)
    mod_rows = _adaln_rows(cond, mod_w, mod_b)
    modtiles = [_tile_mod(mod_rows[i], Bb, tiles_per_batch, n_ctx // ROW_TILE) for i in range(DEPTH)]
    router_w_f32 = jnp.pad(router_w.astype(F32), ((0, 0), (0, LANES - N_EXPERTS)))
    router_w_hi = router_w_f32.astype(BF16)
    router_w_pad = jnp.stack([router_w_hi, (router_w_f32 - router_w_hi.astype(F32)).astype(BF16)])
    router_b_pad = jnp.pad(router_b.astype(F32), (0, LANES - N_EXPERTS))[None]
    rope_att = _rope_tables(n_ctx, S, ATT_HD)
    rope_ret = _rope_tables(n_ctx, S, RET_DK)

    def row(width, col=0, dtype_rows=ROW_TILE):
        return pl.BlockSpec((dtype_rows, width), lambda t: (t, col))

    def vec(width):
        return pl.BlockSpec((1, width), lambda t: (0, 0))

    h = _prenorm(xa, norm_mix[0][None], modtiles[0])
    for i in range(DEPTH):
        kind, j = i % N_MIXERS, i // N_MIXERS
        tm = ROW_TILE
        if kind == 0:
            p = _matmul(h, lru_w_in, layer=j)
            cw = jnp.pad(lru_conv_w[j], ((0, SUBLANES - CONV_W), (0, 0)))
            dirs = [_rglru(p, cw, lru_conv_b[j][None], lru_gate_a_w[j, d].astype(BF16), lru_gate_a_b[j, d][None],
                           lru_gate_x_w[j, d].astype(BF16), lru_gate_x_b[j, d][None], lru_lambda[j, d][None],
                           Bb, n_ctx, S, reverse=(d == 1)) for d in range(2)]
            fin = (_finish_lru, [p] + dirs, [row(LRU_WIDTH, 0), row(LRU_WIDTH), row(LRU_WIDTH)])
            w_out = lru_w_out[j]
        elif kind == 1:
            HK = DN_HEADS * DN_DK
            p = _matmul(h, dn_w_in, layer=j, n_out=4 * HK)
            tail = _matmul(h, jnp.pad(dn_w_in[j][:, 4 * HK:], ((0, 0), (0, LANES - 4 * DN_HEADS))))
            cw = jnp.pad(dn_conv_w[j], ((0, SUBLANES - CONV_W), (0, 0)))
            alog_row = jnp.zeros((1, LANES), F32).at[0, 2 * DN_HEADS:4 * DN_HEADS].set(dn_a_log[j].reshape(-1))
            dtb_row = jnp.zeros((1, LANES), F32).at[0, 2 * DN_HEADS:4 * DN_HEADS].set(dn_dt_bias[j].reshape(-1))
            dirs = [_deltanet(p, tail, cw, alog_row, dtb_row, Bb, n_ctx, S, d) for d in range(2)]
            fin = (_finish_dn, [p] + dirs + [jnp.tile(dn_norm[j], DN_HEADS)[None]],
                   [row(HK, 3), row(HK), row(HK), vec(HK)])
            w_out = dn_w_out[j]
        elif kind == 2:
            tm = ROW_TILE // 2
            HV = RET_HEADS * RET_DV
            p = _matmul(h, ret_w_in, layer=j)
            dirs = [_retention(p, rope_ret[0], rope_ret[1], Bb, n_ctx, S, reverse=(d == 1)) for d in range(2)]
            fin = (_finish_ret, [p] + dirs + [ret_norm[j][None]],
                   [row(HV, 2, tm), row(HV, 0, tm), row(HV, 0, tm), vec(HV)])
            w_out = ret_w_out[j]
        else:
            p = _matmul(h, att_w_in, layer=j)
            q, k, v = _att_prep(p, att_q_norm[j][None], att_k_norm[j][None], rope_att[0], rope_att[1],
                                tiles_per_batch)
            o = _attention(q, k, v, att_sink[j], Bb, n_ctx, S)
            fin = (_finish_att, [o], [row(ATT_HEADS * ATT_HD)])
            w_out = att_w_out[j]

        xa, h2, route = _outproj(fin[0], fin[1], fin[2], w_out.astype(BF16), xa, norm_ffn[i][None],
                                 modtiles[i], router_w_pad, router_b_pad, tm)
        ys = _moe(h2, route, moe_w_gate, moe_w_up, moe_w_down, i)
        if i + 1 < DEPTH:
            xa, h = _resid_norm(xa, ys, route, norm_mix[i + 1][None], modtiles[i], modtiles[i + 1])
        else:
            xa = _resid(xa, ys, route, modtiles[i])
    return xa.reshape(Bb, TB, D)[:, n_ctx:]
```

```python
import functools
import math

import jax
import jax.numpy as jnp
from jax import lax
from jax.experimental import pallas as pl
from jax.experimental.pallas import tpu as pltpu

D_MODEL = 2048
DEPTH = 4
GRID_W = 64
N_MIXERS = 4
NORM_EPS = 1e-6
ROPE_BASE = 10000.0
CONV_W = 4

LRU_WIDTH = D_MODEL
LRU_BLOCKS = 8
LRU_BLOCK = LRU_WIDTH // LRU_BLOCKS
LRU_C = 8.0

DN_HEADS = 16
DN_DK = D_MODEL // DN_HEADS
DN_DV = D_MODEL // DN_HEADS

RET_HEADS = 8
RET_DK = D_MODEL // RET_HEADS
RET_DV = 2 * D_MODEL // RET_HEADS

ATT_HEADS = 16
ATT_KV_HEADS = 4
ATT_HD = D_MODEL // ATT_HEADS
ATT_G = ATT_HEADS // ATT_KV_HEADS
WINDOW = 128

N_EXPERTS = 32
N_GROUPS = 8
EXPERTS_PER_GROUP = N_EXPERTS // N_GROUPS
TOP_K = 2
D_EXPERT = 512
MOE_BLOCK = 256

F32 = jnp.float32
BF16 = jnp.bfloat16

VMEM_LIMIT_BYTES = 56 * 1024 * 1024
SUBLANES = 8
LANES = 128
ROW_TILE = 256
HALO = SUBLANES


def _params(*sem):
    return pltpu.CompilerParams(dimension_semantics=sem, vmem_limit_bytes=VMEM_LIMIT_BYTES)


def _sigmoid(x):
    return 0.5 * jnp.tanh(0.5 * x) + 0.5


def _silu(x):
    return x * _sigmoid(x)


def _softplus(x):
    return jnp.maximum(x, 0.0) + jnp.log(1.0 + jnp.exp(-jnp.abs(x)))


def _mm_kernel(x_ref, w_ref, o_ref, wb_ref):
    @pl.when(pl.program_id(1) == 0)
    def _():
        wb_ref[...] = w_ref[...].astype(BF16)

    o_ref[...] = jnp.dot(x_ref[...], wb_ref[...], preferred_element_type=F32)


def _matmul(x, w, layer=None, n_out=None, tm=512, tn=1024):
    M, K = x.shape
    N = w.shape[-1] if n_out is None else n_out
    tm = min(tm, M)
    tn = min(tn, N)
    assert M % tm == 0 and N % tn == 0, (M, N, tm, tn)
    if layer is None:
        w_spec = pl.BlockSpec((K, tn), lambda n, m: (0, n))
    else:
        w_spec = pl.BlockSpec((None, K, tn), lambda n, m: (layer, 0, n))
    return pl.pallas_call(
        _mm_kernel,
        grid=(N // tn, M // tm),
        in_specs=[pl.BlockSpec((tm, K), lambda n, m: (m, 0)), w_spec],
        out_specs=pl.BlockSpec((tm, tn), lambda n, m: (m, n)),
        out_shape=jax.ShapeDtypeStruct((M, N), F32),
        scratch_shapes=[pltpu.VMEM((K, tn), BF16)],
        compiler_params=_params("arbitrary", "arbitrary"),
        name="dense_matmul",
    )(x, w)


def _mod_kernel(c_ref, w_ref, b_ref, o_ref):
    o_ref[...] = jnp.dot(c_ref[...], w_ref[...].astype(BF16), preferred_element_type=F32) + b_ref[...]


def _adaln_rows(cond, mod_w, mod_b, tn=2048):
    R, D = cond.shape
    L, _, N = mod_w.shape
    return pl.pallas_call(
        _mod_kernel,
        grid=(L, N // tn),
        in_specs=[pl.BlockSpec((R, D), lambda l, n: (0, 0)),
                  pl.BlockSpec((None, D, tn), lambda l, n: (l, 0, n)),
                  pl.BlockSpec((None, 1, tn), lambda l, n: (l, 0, n))],
        out_specs=pl.BlockSpec((None, R, tn), lambda l, n: (l, 0, n)),
        out_shape=jax.ShapeDtypeStruct((L, R, N), F32),
        compiler_params=_params("arbitrary", "arbitrary"),
        name="adaln_rows",
    )(cond, mod_w, mod_b.reshape(L, 1, N))


def _norm_mod(x, normw, shift, scale):
    y = x * lax.rsqrt(jnp.mean(x * x, axis=-1, keepdims=True) + NORM_EPS)
    return (y * normw) * (1.0 + scale) + shift


def _prenorm_kernel(x_ref, nw_ref, mod_ref, h_ref):
    h_ref[...] = _norm_mod(x_ref[...], nw_ref[...], mod_ref[0:1, :], mod_ref[1:2, :]).astype(BF16)


def _prenorm(x, normw, modtile):
    M, D = x.shape
    return pl.pallas_call(
        _prenorm_kernel,
        grid=(M // ROW_TILE,),
        in_specs=[pl.BlockSpec((ROW_TILE, D), lambda t: (t, 0)),
                  pl.BlockSpec((1, D), lambda t: (0, 0)),
                  pl.BlockSpec((None, 8, D), lambda t: (t, 0, 0))],
        out_specs=pl.BlockSpec((ROW_TILE, D), lambda t: (t, 0)),
        out_shape=jax.ShapeDtypeStruct((M, D), BF16),
        compiler_params=_params("arbitrary"),
        name="prenorm",
    )(x, normw, modtile)


def _ffn_residual(x_ref, y_refs, rt_ref, mod_ref):
    f = None
    for k, y_ref in enumerate(y_refs):
        lane = TOP_K + k
        term = y_ref[...] * rt_ref[:, lane:lane + 1]
        f = term if f is None else f + term
    return x_ref[...] + mod_ref[5:6, :] * f


def _resid_norm_kernel(x_ref, y0_ref, y1_ref, rt_ref, nw_ref, mod_ref, modn_ref, xo_ref, h_ref):
    x = _ffn_residual(x_ref, (y0_ref, y1_ref), rt_ref, mod_ref)
    xo_ref[...] = x
    h_ref[...] = _norm_mod(x, nw_ref[...], modn_ref[0:1, :], modn_ref[1:2, :]).astype(BF16)


def _resid_norm(x, ys, route, normw_next, modtile, modtile_next):
    M, D = x.shape
    row = pl.BlockSpec((ROW_TILE, D), lambda t: (t, 0))
    mod = pl.BlockSpec((None, 8, D), lambda t: (t, 0, 0))
    return pl.pallas_call(
        _resid_norm_kernel,
        grid=(M // ROW_TILE,),
        in_specs=[row, row, row, pl.BlockSpec((ROW_TILE, LANES), lambda t: (t, 0)),
                  pl.BlockSpec((1, D), lambda t: (0, 0)), mod, mod],
        out_specs=[row, row],
        out_shape=[jax.ShapeDtypeStruct((M, D), F32), jax.ShapeDtypeStruct((M, D), BF16)],
        compiler_params=_params("arbitrary"),
        name="resid_norm",
    )(x, ys[0], ys[1], route, normw_next, modtile, modtile_next)


def _resid_kernel(x_ref, y0_ref, y1_ref, rt_ref, mod_ref, xo_ref):
    xo_ref[...] = _ffn_residual(x_ref, (y0_ref, y1_ref), rt_ref, mod_ref)


def _resid(x, ys, route, modtile):
    M, D = x.shape
    row = pl.BlockSpec((ROW_TILE, D), lambda t: (t, 0))
    return pl.pallas_call(
        _resid_kernel,
        grid=(M // ROW_TILE,),
        in_specs=[row, row, row, pl.BlockSpec((ROW_TILE, LANES), lambda t: (t, 0)),
                  pl.BlockSpec((None, 8, D), lambda t: (t, 0, 0))],
        out_specs=row,
        out_shape=jax.ShapeDtypeStruct((M, D), F32),
        compiler_params=_params("arbitrary"),
        name="resid",
    )(x, ys[0], ys[1], route, modtile)


def _head_rms(o, width, normw):
    parts = []
    for h in range(o.shape[1] // width):
        oh = o[:, h * width:(h + 1) * width]
        parts.append(oh * lax.rsqrt(jnp.mean(oh * oh, axis=-1, keepdims=True) + NORM_EPS))
    return jnp.concatenate(parts, axis=1) * normw


def _both_directions(f_ref, b_ref):
    return f_ref[...].astype(F32) + b_ref[...].astype(F32)


def _finish_lru(gate_ref, hf_ref, hb_ref):
    return jax.nn.gelu(gate_ref[...]) * _both_directions(hf_ref, hb_ref)


def _finish_dn(z_ref, of_ref, ob_ref, nw_ref):
    return _head_rms(_both_directions(of_ref, ob_ref), DN_DV, nw_ref[...]) * _silu(z_ref[...])


def _finish_ret(gate_ref, of_ref, ob_ref, nw_ref):
    return _head_rms(_both_directions(of_ref, ob_ref), RET_DV, nw_ref[...]) * _silu(gate_ref[...])


def _finish_att(o_ref):
    return o_ref[...]


ROUTE_WEIGHT_LANE = TOP_K


def _route_tile(logits, bias):
    assert TOP_K == 2 and EXPERTS_PER_GROUP == 4
    lane = lax.broadcasted_iota(jnp.int32, logits.shape, 1)
    valid = lane < N_EXPERTS
    neg = -jnp.inf
    scores = _sigmoid(logits)
    b = jnp.where(valid, scores + bias, neg)
    j = lane % EXPERTS_PER_GROUP
    jf = j.astype(F32)
    gf = (lane // EXPERTS_PER_GROUP).astype(F32)

    def group_reduce(v, op):
        v = op(v, jnp.where(j % 2 == 0, pltpu.roll(v, LANES - 1, axis=1), pltpu.roll(v, 1, axis=1)))
        return op(v, jnp.where(j < 2, pltpu.roll(v, LANES - 2, axis=1), pltpu.roll(v, 2, axis=1)))

    m1 = group_reduce(b, jnp.maximum)
    is1 = jf == group_reduce(jnp.where(b == m1, jf, float(EXPERTS_PER_GROUP)), jnp.minimum)
    b2 = jnp.where(is1, neg, b)
    m2 = group_reduce(b2, jnp.maximum)
    is2 = jf == group_reduce(jnp.where(b2 == m2, jf, float(EXPERTS_PER_GROUP)), jnp.minimum)
    gscore = jnp.where(valid, m1 + m2, neg)
    best = jnp.max(gscore, axis=-1, keepdims=True)
    gsel = jnp.min(jnp.where(gscore == best, gf, float(N_GROUPS)), axis=-1, keepdims=True)
    in_group = valid & (gf == gsel)
    lf = lane.astype(F32)

    def pick(mask, v):
        return jnp.sum(jnp.where(in_group & mask, v, 0.0), axis=-1, keepdims=True)

    e1, e2 = pick(is1, lf), pick(is2, lf)
    s1, s2 = pick(is1, scores), pick(is2, scores)
    tot = s1 + s2
    return jnp.where(lane == 0, e1, jnp.where(lane == 1, e2, jnp.where(
        lane == ROUTE_WEIGHT_LANE, s1 / tot, jnp.where(lane == ROUTE_WEIGHT_LANE + 1, s2 / tot, 0.0))))


def _outproj_kernel(finish, n_fin, *refs):
    fin_refs = refs[:n_fin]
    w_ref, x_ref, nw_ref, mod_ref, rw_ref, rb_ref, xo_ref, h_ref, rt_ref = refs[n_fin:]
    a = finish(*fin_refs).astype(BF16)
    y = jnp.dot(a, w_ref[...], preferred_element_type=F32)
    x = x_ref[...] + mod_ref[2:3, :] * y
    xo_ref[...] = x
    h = _norm_mod(x, nw_ref[...], mod_ref[3:4, :], mod_ref[4:5, :])
    h_ref[...] = h
    h_hi = h.astype(BF16)
    h_lo = (h - h_hi.astype(F32)).astype(BF16)
    logits = (jnp.dot(h_hi, rw_ref[0], preferred_element_type=F32)
              + jnp.dot(h_lo, rw_ref[0], preferred_element_type=F32)
              + jnp.dot(h_hi, rw_ref[1], preferred_element_type=F32))
    rt_ref[...] = _route_tile(logits, rb_ref[...])


def _outproj(finish, fin_args, fin_specs, w_bf16, x, normw_ffn, modtile, router_w_pad, router_b_pad, tm):
    M, D = x.shape
    K = w_bf16.shape[0]
    per = ROW_TILE // tm
    row = pl.BlockSpec((tm, D), lambda t: (t, 0))
    in_specs = list(fin_specs) + [
        pl.BlockSpec((K, D), lambda t: (0, 0)),
        row,
        pl.BlockSpec((1, D), lambda t: (0, 0)),
        pl.BlockSpec((None, 8, D), lambda t: (t // per, 0, 0)),
        pl.BlockSpec((2, D, LANES), lambda t: (0, 0, 0)),
        pl.BlockSpec((1, LANES), lambda t: (0, 0)),
    ]
    return pl.pallas_call(
        functools.partial(_outproj_kernel, finish, len(fin_args)),
        grid=(M // tm,),
        in_specs=in_specs,
        out_specs=[row, row, pl.BlockSpec((tm, LANES), lambda t: (t, 0))],
        out_shape=[jax.ShapeDtypeStruct((M, D), F32), jax.ShapeDtypeStruct((M, D), F32),
                   jax.ShapeDtypeStruct((M, LANES), F32)],
        compiler_params=_params("arbitrary"),
        name="outproj",
    )(*fin_args, w_bf16, x, normw_ffn, modtile, router_w_pad, router_b_pad)


def _rope_tables(n_ctx, seq, head_dim):
    quarter = head_dim // 4
    pos = jnp.arange(seq)
    inv_freq = ROPE_BASE ** (-jnp.arange(quarter, dtype=F32) / quarter)
    ang_r = (pos // GRID_W).astype(F32)[:, None] * inv_freq
    ang_c = (pos % GRID_W).astype(F32)[:, None] * inv_freq
    c = jnp.concatenate([jnp.cos(ang_r), jnp.cos(ang_r), jnp.cos(ang_c), jnp.cos(ang_c)], axis=1)
    s = jnp.concatenate([-jnp.sin(ang_r), jnp.sin(ang_r), -jnp.sin(ang_c), jnp.sin(ang_c)], axis=1)
    c = jnp.concatenate([jnp.ones((n_ctx, head_dim), F32), c], axis=0)
    s = jnp.concatenate([jnp.zeros((n_ctx, head_dim), F32), s], axis=0)
    return c, s


def _rope(x, c, s, quarter):
    width = x.shape[1]
    if 2 * quarter == LANES:
        parts = [pltpu.roll(x[:, j:j + LANES], quarter, axis=1) for j in range(0, width, LANES)]
        partner = parts[0] if len(parts) == 1 else jnp.concatenate(parts, axis=1)
    else:
        assert width == LANES and 4 * quarter == LANES
        lane = lax.broadcasted_iota(jnp.int32, x.shape, 1)
        partner = jnp.where((lane % (2 * quarter)) < quarter,
                            pltpu.roll(x, LANES - quarter, axis=1), pltpu.roll(x, quarter, axis=1))
    return x * c + partner * s


ATT_TILE = 128


def _att_prep_kernel(p_ref, qn_ref, kn_ref, c_ref, s_ref, q_ref, k_ref, v_ref):
    c = c_ref[...]
    s = s_ref[...]
    qw = ATT_HEADS * ATT_HD
    kw = ATT_KV_HEADS * ATT_HD

    def norm_rope(xh, w):
        y = xh * lax.rsqrt(jnp.mean(xh * xh, axis=-1, keepdims=True) + NORM_EPS) * w
        return _rope(y, c, s, ATT_HD // 4)

    for h in range(ATT_HEADS):
        qh = norm_rope(p_ref[:, h * ATT_HD:(h + 1) * ATT_HD], qn_ref[...])
        q_ref[:, h * ATT_HD:(h + 1) * ATT_HD] = (qh * ATT_HD ** -0.5).astype(BF16)
    for h in range(ATT_KV_HEADS):
        kh = norm_rope(p_ref[:, qw + h * ATT_HD:qw + (h + 1) * ATT_HD], kn_ref[...])
        k_ref[:, h * ATT_HD:(h + 1) * ATT_HD] = kh.astype(BF16)
    v_ref[...] = p_ref[:, qw + kw:qw + 2 * kw].astype(BF16)


def _att_prep(p, q_norm, k_norm, rope_c, rope_s, tiles_per_batch):
    M = p.shape[0]
    qw, kw = ATT_HEADS * ATT_HD, ATT_KV_HEADS * ATT_HD
    tab = pl.BlockSpec((ROW_TILE, ATT_HD), lambda t: (t % tiles_per_batch, 0))
    return pl.pallas_call(
        _att_prep_kernel,
        grid=(M // ROW_TILE,),
        in_specs=[pl.BlockSpec((ROW_TILE, qw + 2 * kw), lambda t: (t, 0)),
                  pl.BlockSpec((1, ATT_HD), lambda t: (0, 0)),
                  pl.BlockSpec((1, ATT_HD), lambda t: (0, 0)), tab, tab],
        out_specs=[pl.BlockSpec((ROW_TILE, qw), lambda t: (t, 0)),
                   pl.BlockSpec((ROW_TILE, kw), lambda t: (t, 0)),
                   pl.BlockSpec((ROW_TILE, kw), lambda t: (t, 0))],
        out_shape=[jax.ShapeDtypeStruct((M, qw), BF16), jax.ShapeDtypeStruct((M, kw), BF16),
                   jax.ShapeDtypeStruct((M, kw), BF16)],
        compiler_params=_params("arbitrary"),
        name="att_prep",
    )(p, q_norm, k_norm, rope_c, rope_s)


def _att_kernel(n_ctx_tiles, n_tiles, q_ref, kp_ref, ko_ref, kn_ref, vp_ref, vo_ref, vn_ref,
                kc_ref, vc_ref, sink_ref, o_ref):
    t = pl.program_id(1)
    T = ATT_TILE
    rows = ATT_G * T
    row = lax.broadcasted_iota(jnp.int32, (rows, 3 * T), 0) % T
    col = lax.broadcasted_iota(jnp.int32, (rows, 3 * T), 1)
    blk = col // T
    c = col % T
    latent = t >= n_ctx_tiles
    prev_ok = latent & (t >= n_ctx_tiles + 1)
    next_ok = latent & (t <= n_tiles - 2)
    band = ((blk == 1) & latent) | ((blk == 0) & (c >= row) & prev_ok) | ((blk == 2) & (c <= row) & next_ok)
    nt = (((1,), (1,)), ((), ()))
    KV = range(ATT_KV_HEADS)
    ks = [slice(kvh * ATT_HD, (kvh + 1) * ATT_HD) for kvh in KV]
    q4 = [jnp.concatenate([q_ref[:, (kvh * ATT_G + g) * ATT_HD:(kvh * ATT_G + g + 1) * ATT_HD]
                           for g in range(ATT_G)], axis=0) for kvh in KV]
    s_loc = [lax.dot_general(q4[i], jnp.concatenate([kp_ref[:, ks[i]], ko_ref[:, ks[i]], kn_ref[:, ks[i]]], axis=0),
                             nt, preferred_element_type=F32) for i in KV]
    s_ctx = [lax.dot_general(q4[i], kc_ref[:, ks[i]], nt, preferred_element_type=F32) for i in KV]
    s_loc = [jnp.where(band, s_loc[i], -jnp.inf) for i in KV]
    sink = [sink_ref[i][:, 0:1] for i in KV]
    m = [jnp.maximum(sink[i], jnp.maximum(jnp.max(s_loc[i], axis=-1, keepdims=True),
                                          jnp.max(s_ctx[i], axis=-1, keepdims=True))) for i in KV]
    p_loc = [jnp.exp(s_loc[i] - m[i]) for i in KV]
    p_ctx = [jnp.exp(s_ctx[i] - m[i]) for i in KV]
    den = [jnp.exp(sink[i] - m[i]) + jnp.sum(p_loc[i], axis=-1, keepdims=True)
           + jnp.sum(p_ctx[i], axis=-1, keepdims=True) for i in KV]
    o = [jnp.dot(p_loc[i].astype(BF16),
                 jnp.concatenate([vp_ref[:, ks[i]], vo_ref[:, ks[i]], vn_ref[:, ks[i]]], axis=0),
                 preferred_element_type=F32)
         + jnp.dot(p_ctx[i].astype(BF16), vc_ref[:, ks[i]], preferred_element_type=F32) for i in KV]
    for i in KV:
        oi = o[i] / den[i]
        for g in range(ATT_G):
            h = i * ATT_G + g
            o_ref[:, h * ATT_HD:(h + 1) * ATT_HD] = oi[g * T:(g + 1) * T, :].astype(BF16)


def _attention(q, k, v, sink, n_batch, n_ctx, seq):
    M, qw = q.shape
    kw = k.shape[1]
    T = ATT_TILE
    n_tiles = (n_ctx + seq) // T
    n_ctx_tiles = n_ctx // T
    sink_rows = jnp.broadcast_to(
        jnp.repeat(sink.astype(F32).reshape(ATT_KV_HEADS, ATT_G), T, axis=1)[:, :, None],
        (ATT_KV_HEADS, ATT_G * T, LANES))

    def tile(off):
        return pl.BlockSpec((T, kw), lambda b, t: (b * n_tiles + jnp.clip(t + off, 0, n_tiles - 1), 0))

    ctx = pl.BlockSpec((n_ctx, kw), lambda b, t: (b * (n_tiles // n_ctx_tiles), 0))
    return pl.pallas_call(
        functools.partial(_att_kernel, n_ctx_tiles, n_tiles),
        grid=(n_batch, n_tiles),
        in_specs=[pl.BlockSpec((T, qw), lambda b, t: (b * n_tiles + t, 0)),
                  tile(-1), tile(0), tile(1), tile(-1), tile(0), tile(1), ctx, ctx,
                  pl.BlockSpec((ATT_KV_HEADS, ATT_G * T, LANES), lambda b, t: (0, 0, 0))],
        out_specs=pl.BlockSpec((T, qw), lambda b, t: (b * n_tiles + t, 0)),
        out_shape=jax.ShapeDtypeStruct((M, qw), BF16),
        compiler_params=_params("arbitrary", "arbitrary"),
        name="window_attention",
    )(q, k, k, k, v, v, v, k, v, sink_rows)


def _chunk_of_step(s, n_chunks, n_ctx_chunks, reverse):
    if not reverse:
        return s
    return jnp.where(s < n_ctx_chunks, n_ctx_chunks - 1 - s, n_chunks - 1 - (s - n_ctx_chunks))


RET_CHUNK = ROW_TILE


def _ret_kernel(reverse, q_ref, k_ref, v_ref, c_ref, s_ref, o_ref, state_ref):
    @pl.when(pl.program_id(1) == 0)
    def _():
        state_ref[...] = jnp.zeros_like(state_ref)

    C = RET_CHUNK
    c = c_ref[...]
    s = s_ref[...]
    i = lax.broadcasted_iota(jnp.int32, (C, C), 0)
    j = lax.broadcasted_iota(jnp.int32, (C, C), 1)
    pos = lax.broadcasted_iota(jnp.int32, (C, 1), 0).astype(F32)
    diff = (j - i) if reverse else (i - j)
    mask = diff > 0 if reverse else diff >= 0
    dist = jnp.maximum(diff, 0).astype(F32)
    steps_in = (C - pos) if reverse else (pos + 1.0)
    steps_out = pos if reverse else (C - 1.0 - pos)
    nt = (((1,), (1,)), ((), ()))
    for h in range(RET_HEADS):
        log_gamma = math.log1p(-2.0 ** (-5.0 - h))
        q = _rope(q_ref[:, h * RET_DK:(h + 1) * RET_DK], c, s, RET_DK // 4)
        k = _rope(k_ref[:, h * RET_DK:(h + 1) * RET_DK], c, s, RET_DK // 4) * RET_DK ** -0.5
        v = v_ref[:, h * RET_DV:(h + 1) * RET_DV].astype(BF16)
        qb = q.astype(BF16)
        dmat = jnp.where(mask, jnp.exp(dist * log_gamma), 0.0)
        a = lax.dot_general(qb, k.astype(BF16), nt, preferred_element_type=F32) * dmat
        st = state_ref[h]
        o = jnp.dot(a.astype(BF16), v, preferred_element_type=F32)
        o = o + jnp.dot(qb, st.astype(BF16), preferred_element_type=F32) * jnp.exp(steps_in * log_gamma)
        o_ref[:, h * RET_DV:(h + 1) * RET_DV] = o.astype(BF16)
        kt = (k * jnp.exp(steps_out * log_gamma)).T.astype(BF16)
        state_ref[h] = st * math.exp(C * log_gamma) + jnp.dot(kt, v, preferred_element_type=F32)


def _retention(p, rope_c, rope_s, n_batch, n_ctx, seq, reverse):
    M = p.shape[0]
    C = RET_CHUNK
    n_chunks = (n_ctx + seq) // C
    n_ctx_chunks = n_ctx // C
    hk, hv = RET_HEADS * RET_DK, RET_HEADS * RET_DV

    def chunk(b, s):
        return _chunk_of_step(s, n_chunks, n_ctx_chunks, reverse)

    tab = pl.BlockSpec((C, RET_DK), lambda b, s: (chunk(b, s), 0))
    return pl.pallas_call(
        functools.partial(_ret_kernel, reverse),
        grid=(n_batch, n_chunks),
        in_specs=[pl.BlockSpec((C, hk), lambda b, s: (b * n_chunks + chunk(b, s), 0)),
                  pl.BlockSpec((C, hk), lambda b, s: (b * n_chunks + chunk(b, s), 1)),
                  pl.BlockSpec((C, hv), lambda b, s: (b * n_chunks + chunk(b, s), 1)),
                  tab, tab],
        out_specs=pl.BlockSpec((C, hv), lambda b, s: (b * n_chunks + chunk(b, s), 0)),
        out_shape=jax.ShapeDtypeStruct((M, hv), BF16),
        scratch_shapes=[pltpu.VMEM((RET_HEADS, RET_DK, RET_DV), F32)],
        compiler_params=_params("arbitrary", "arbitrary"),
        name="retention_rev" if reverse else "retention_fwd",
    )(p, p, p, rope_c, rope_s)


LRU_CHUNK = ROW_TILE


def _lru_kernel(reverse, n_chunks, n_ctx_chunks, x_ref, xp_ref, xn_ref, cw_ref, cb_ref, wa_ref, ba_ref,
                wx_ref, bx_ref, lam_ref, o_ref, h_ref):
    step = pl.program_id(1)

    @pl.when(step == 0)
    def _():
        h_ref[...] = jnp.zeros_like(h_ref)

    C = LRU_CHUNK
    W = LRU_BLOCK
    chunk = _chunk_of_step(step, n_chunks, n_ctx_chunks, reverse)
    has_prev = ((chunk != 0) & (chunk != n_ctx_chunks)).astype(F32)
    has_next = ((chunk != n_ctx_chunks - 1) & (chunk != n_chunks - 1)).astype(F32)
    rows = C + 2 * HALO
    sub = lax.broadcasted_iota(jnp.int32, (C, W), 0) % SUBLANES

    def block(n, carry):
        lanes = pl.ds(pl.multiple_of(n * W, W), W)
        full = jnp.concatenate([xp_ref[:, lanes] * has_prev, x_ref[:, lanes], xn_ref[:, lanes] * has_next], axis=0)
        cw = cw_ref[:, lanes]
        xc = (cw[0:1] * pltpu.roll(full, 2, axis=0)[HALO:HALO + C]
              + cw[1:2] * pltpu.roll(full, 1, axis=0)[HALO:HALO + C]
              + cw[2:3] * full[HALO:HALO + C]
              + cw[3:4] * pltpu.roll(full, rows - 1, axis=0)[HALO:HALO + C]) + cb_ref[:, lanes]
        xb = xc.astype(BF16)
        r = _sigmoid(jnp.dot(xb, wa_ref[n], preferred_element_type=F32) + ba_ref[:, lanes])
        gi = _sigmoid(jnp.dot(xb, wx_ref[n], preferred_element_type=F32) + bx_ref[:, lanes])
        a = jnp.exp((-LRU_C) * r * _softplus(-lam_ref[:, lanes]))
        b = jnp.sqrt(jnp.maximum(1.0 - a * a, 0.0)) * (gi * xc)
        for k in (1, 2, 4):
            if reverse:
                keep = sub < SUBLANES - k
                a_sh = pltpu.roll(a, C - k, axis=0)
                b_sh = pltpu.roll(b, C - k, axis=0)
            else:
                keep = sub >= k
                a_sh = pltpu.roll(a, k, axis=0)
                b_sh = pltpu.roll(b, k, axis=0)
            b = b + a * jnp.where(keep, b_sh, 0.0)
            a = a * jnp.where(keep, a_sh, 1.0)
        hin = h_ref[0:1, lanes]
        groups = C // SUBLANES
        outs = [None] * groups
        for g in (range(groups - 1, -1, -1) if reverse else range(groups)):
            lo = g * SUBLANES
            hg = b[lo:lo + SUBLANES] + a[lo:lo + SUBLANES] * hin
            outs[g] = hg
            hin = hg[0:1] if reverse else hg[SUBLANES - 1:SUBLANES]
        o_ref[:, lanes] = jnp.concatenate(outs, axis=0).astype(BF16)
        h_ref[:, lanes] = jnp.broadcast_to(hin, (SUBLANES, W))
        return carry

    lax.fori_loop(0, LRU_BLOCKS, block, 0)


def _rglru(p, conv_w, conv_b, gate_a_w, gate_a_b, gate_x_w, gate_x_b, lam, n_batch, n_ctx, seq, reverse):
    M = p.shape[0]
    C = LRU_CHUNK
    Wd = LRU_WIDTH
    n_chunks = (n_ctx + seq) // C
    n_ctx_chunks = n_ctx // C
    per = C // HALO
    last = M // HALO - 1

    def tile(b, s):
        return b * n_chunks + _chunk_of_step(s, n_chunks, n_ctx_chunks, reverse)

    vec = pl.BlockSpec((1, Wd), lambda b, s: (0, 0))
    wts = pl.BlockSpec((LRU_BLOCKS, LRU_BLOCK, LRU_BLOCK), lambda b, s: (0, 0, 0))
    return pl.pallas_call(
        functools.partial(_lru_kernel, reverse, n_chunks, n_ctx_chunks),
        grid=(n_batch, n_chunks),
        in_specs=[pl.BlockSpec((C, Wd), lambda b, s: (tile(b, s), 1)),
                  pl.BlockSpec((HALO, Wd), lambda b, s: (jnp.maximum(tile(b, s) * per - 1, 0), 1)),
                  pl.BlockSpec((HALO, Wd), lambda b, s: (jnp.minimum((tile(b, s) + 1) * per, last), 1)),
                  pl.BlockSpec((SUBLANES, Wd), lambda b, s: (0, 0)),
                  vec, wts, vec, wts, vec, vec],
        out_specs=pl.BlockSpec((C, Wd), lambda b, s: (tile(b, s), 0)),
        out_shape=jax.ShapeDtypeStruct((M, Wd), BF16),
        scratch_shapes=[pltpu.VMEM((SUBLANES, Wd), F32)],
        compiler_params=_params("arbitrary", "arbitrary"),
        name="rglru_rev" if reverse else "rglru_fwd",
    )(p, p, p, conv_w, conv_b, gate_a_w, gate_a_b, gate_x_w, gate_x_b, lam)


DN_CHUNK = 128
DN_HEAD_GROUP = 8
DN_INV_BASE = SUBLANES
DN_NEUMANN_ROUNDS = 2


def _dn_kernel(reverse, d, n_chunks, n_ctx_chunks, q_ref, k_ref, v_ref, xp_ref, xn_ref, tail_ref, cw_ref,
               alog_ref, dtb_ref, o_ref, state_ref, gct_ref, bt_ref):
    step = pl.program_id(1)

    @pl.when(step == 0)
    def _():
        state_ref[...] = jnp.zeros_like(state_ref)

    C = DN_CHUNK
    H = DN_HEADS
    HK = H * DN_DK
    chunk = _chunk_of_step(step, n_chunks, n_ctx_chunks, reverse)
    has_prev = ((chunk != 0) & (chunk != n_ctx_chunks)).astype(F32)
    has_next = ((chunk != n_ctx_chunks - 1) & (chunk != n_chunks - 1)).astype(F32)
    rows = C + 2 * HALO
    ci = lax.broadcasted_iota(jnp.int32, (C, C), 0)
    si = lax.broadcasted_iota(jnp.int32, (C, C), 1)
    mask = (si >= ci) if reverse else (si <= ci)
    strict = (si > ci) if reverse else (si < ci)
    eye = (si == ci).astype(F32)
    last = 0 if reverse else C - 1
    same_base = (ci // DN_INV_BASE) == (si // DN_INV_BASE)
    merge_masks = []
    size = DN_INV_BASE
    while size < C:
        cb, sb = ci // size, si // size
        merge_masks.append(((cb % 2 == 0) & (sb == cb + 1)) if reverse else ((cb % 2 == 1) & (sb == cb - 1)))
        size *= 2

    tail = tail_ref[...]
    beta = _sigmoid(tail)
    g = -jnp.exp(alog_ref[...]) * _softplus(tail + dtb_ref[...])
    gcum = jnp.dot(mask.astype(F32), g, preferred_element_type=F32, precision=lax.Precision.HIGHEST)
    gct_ref[...] = gcum.T
    bt_ref[...] = beta.T
    nt = (((1,), (1,)), ((), ()))

    def conv_silu(ref, ref_off, h):
        lanes_in = pl.ds(pl.multiple_of(h * DN_DK, DN_DK), DN_DK)
        lanes_all = pl.ds(pl.multiple_of(ref_off + h * DN_DK, DN_DK), DN_DK)
        full = jnp.concatenate([xp_ref[:, lanes_all] * has_prev, ref[:, lanes_in], xn_ref[:, lanes_all] * has_next],
                               axis=0)
        cw = cw_ref[:, lanes_all]
        y = (cw[0:1] * pltpu.roll(full, 2, axis=0)[HALO:HALO + C]
             + cw[1:2] * pltpu.roll(full, 1, axis=0)[HALO:HALO + C]
             + cw[2:3] * full[HALO:HALO + C]
             + cw[3:4] * pltpu.roll(full, rows - 1, axis=0)[HALO:HALO + C])
        return _silu(y)

    def l2n(x):
        return x * lax.rsqrt(jnp.sum(x * x, axis=-1, keepdims=True) + NORM_EPS)

    def dot(a, b):
        return jnp.dot(a.astype(BF16), b.astype(BF16), preferred_element_type=F32)

    def dot_nt(a, b):
        return lax.dot_general(a.astype(BF16), b.astype(BF16), nt, preferred_element_type=F32)

    def group(gi, carry):
        hs = [gi * DN_HEAD_GROUP + jj for jj in range(DN_HEAD_GROUP)]
        G = range(DN_HEAD_GROUP)
        sts = [state_ref[h] for h in hs]
        g_row = [gct_ref[pl.ds(2 * H + d * H + h, 1), :] for h in hs]
        g_rows = [jnp.broadcast_to(g_row[i], (C, C)) for i in G]
        g_cols = [g_rows[i].T for i in G]
        b_cols = [jnp.broadcast_to(bt_ref[pl.ds(d * H + h, 1), :], (C, C)).T for h in hs]
        g_last = [g_row[i][:, last:last + 1] for i in G]
        decay = [jnp.where(mask, jnp.exp(jnp.where(mask, g_cols[i] - g_rows[i], 0.0)), 0.0) for i in G]
        q = [l2n(conv_silu(q_ref, 0, h)) * DN_DK ** -0.5 for h in hs]
        k = [l2n(conv_silu(k_ref, HK, h)) for h in hs]
        v = [conv_silu(v_ref, 2 * HK, h) for h in hs]
        kb = [k[i] * b_cols[i] for i in G]
        m = [jnp.where(strict, dot_nt(kb[i], k[i]) * decay[i], 0.0) for i in G]
        a_intra = [dot_nt(q[i], k[i]) * decay[i] for i in G]

        n = [jnp.where(same_base, -m[i], 0.0) for i in G]
        t = [eye + n[i] for i in G]
        for _ in range(DN_NEUMANN_ROUNDS):
            n = [dot(n[i], n[i]) for i in G]
            t = [t[i] + dot(t[i], n[i]) for i in G]
        for pair in merge_masks:
            x = [dot(t[i], jnp.where(pair, m[i], 0.0)) for i in G]
            t = [t[i] - dot(x[i], t[i]) for i in G]
        sol = [dot(t[i], jnp.concatenate([v[i] * b_cols[i], kb[i] * jnp.exp(g_cols[i])], axis=1)) for i in G]

        r = [dot(jnp.concatenate([sol[i][:, DN_DV:], q[i] * jnp.exp(g_cols[i])], axis=0), sts[i]) for i in G]
        v_new = [sol[i][:, :DN_DV] - r[i][:C] for i in G]
        o = [r[i][C:] + dot(a_intra[i], v_new[i]) for i in G]
        st_new = [sts[i] * jnp.exp(g_last[i]) + dot((k[i] * jnp.exp(g_last[i] - g_cols[i])).T, v_new[i]) for i in G]
        for i, h in enumerate(hs):
            o_ref[:, pl.ds(pl.multiple_of(h * DN_DV, DN_DV), DN_DV)] = o[i].astype(BF16)
            state_ref[h] = st_new[i]
        return carry

    lax.fori_loop(0, H // DN_HEAD_GROUP, group, 0)


def _deltanet(p, tail, conv_w, alog_row, dtb_row, n_batch, n_ctx, seq, d):
    M = p.shape[0]
    C = DN_CHUNK
    reverse = d == 1
    n_chunks = (n_ctx + seq) // C
    n_ctx_chunks = n_ctx // C
    HK = DN_HEADS * DN_DK
    per = C // HALO
    last = M // HALO - 1

    def tile(b, s):
        return b * n_chunks + _chunk_of_step(s, n_chunks, n_ctx_chunks, reverse)

    row1 = pl.BlockSpec((1, LANES), lambda b, s: (0, 0))
    return pl.pallas_call(
        functools.partial(_dn_kernel, reverse, d, n_chunks, n_ctx_chunks),
        grid=(n_batch, n_chunks),
        in_specs=[pl.BlockSpec((C, HK), lambda b, s: (tile(b, s), 0)),
                  pl.BlockSpec((C, HK), lambda b, s: (tile(b, s), 1)),
                  pl.BlockSpec((C, HK), lambda b, s: (tile(b, s), 2)),
                  pl.BlockSpec((HALO, 4 * HK), lambda b, s: (jnp.maximum(tile(b, s) * per - 1, 0), 0)),
                  pl.BlockSpec((HALO, 4 * HK), lambda b, s: (jnp.minimum((tile(b, s) + 1) * per, last), 0)),
                  pl.BlockSpec((C, LANES), lambda b, s: (tile(b, s), 0)),
                  pl.BlockSpec((SUBLANES, 3 * HK), lambda b, s: (0, 0)),
                  row1, row1],
        out_specs=pl.BlockSpec((C, HK), lambda b, s: (tile(b, s), 0)),
        out_shape=jax.ShapeDtypeStruct((M, HK), BF16),
        scratch_shapes=[pltpu.VMEM((DN_HEADS, DN_DK, DN_DV), F32),
                        pltpu.VMEM((LANES, C), F32), pltpu.VMEM((LANES, C), F32)],
        compiler_params=_params("arbitrary", "arbitrary"),
        name="deltanet_rev" if reverse else "deltanet_fwd",
    )(p, p, p, p, p, tail, conv_w, alog_row, dtb_row)


def _moe_kernel(be_ref, nu_ref, xs_ref, wg_ref, wu_ref, wd_ref, o_ref, wgb, wub, wdb):
    b = pl.program_id(0)
    e = be_ref[b]
    prev = be_ref[jnp.maximum(b - 1, 0)]
    used = b < nu_ref[0]

    @pl.when(used & ((b == 0) | (e != prev)))
    def _():
        wgb[...] = wg_ref[...].astype(BF16)
        wub[...] = wu_ref[...].astype(BF16)
        wdb[...] = wd_ref[...].astype(BF16)

    @pl.when(used)
    def _():
        x = xs_ref[...].astype(BF16)
        g = jnp.dot(x, wgb[...], preferred_element_type=F32)
        u = jnp.dot(x, wub[...], preferred_element_type=F32)
        a = (_silu(g) * u).astype(BF16)
        o_ref[...] = jnp.dot(a, wdb[...], preferred_element_type=F32)

    @pl.when(jnp.logical_not(used))
    def _():
        o_ref[...] = jnp.zeros_like(o_ref)


def _moe_experts(xs, block_expert, n_used, w_gate, w_up, w_down, layer):
    n_slots, D = xs.shape
    n_blocks = n_slots // MOE_BLOCK
    DE = w_gate.shape[-1]
    grid_spec = pltpu.PrefetchScalarGridSpec(
        num_scalar_prefetch=2,
        grid=(n_blocks,),
        in_specs=[pl.BlockSpec((MOE_BLOCK, D), lambda b, be, nu: (b, 0)),
                  pl.BlockSpec((None, None, D, DE), lambda b, be, nu: (layer, be[b], 0, 0)),
                  pl.BlockSpec((None, None, D, DE), lambda b, be, nu: (layer, be[b], 0, 0)),
                  pl.BlockSpec((None, None, DE, D), lambda b, be, nu: (layer, be[b], 0, 0))],
        out_specs=pl.BlockSpec((MOE_BLOCK, D), lambda b, be, nu: (b, 0)),
        scratch_shapes=[pltpu.VMEM((D, DE), BF16), pltpu.VMEM((D, DE), BF16), pltpu.VMEM((DE, D), BF16)],
    )
    return pl.pallas_call(
        _moe_kernel,
        grid_spec=grid_spec,
        out_shape=jax.ShapeDtypeStruct((n_slots, D), F32),
        compiler_params=_params("arbitrary"),
        name="moe_experts",
    )(block_expert.astype(jnp.int32), n_used.astype(jnp.int32).reshape(1), xs, w_gate, w_up, w_down)


RANK_BLOCK = 128


def _moe(h2, route, w_gate, w_up, w_down, layer):
    N = h2.shape[0]
    NK = N * TOP_K
    assert NK % RANK_BLOCK == 0
    flat_e = route[:, :TOP_K].astype(jnp.int32).reshape(NK)
    onehot = flat_e[:, None] == jnp.arange(N_EXPERTS, dtype=jnp.int32)[None, :]
    oh = onehot.astype(BF16).reshape(NK // RANK_BLOCK, RANK_BLOCK, N_EXPERTS)
    tri = jnp.tril(jnp.ones((RANK_BLOCK, RANK_BLOCK), BF16))
    within = jnp.einsum('ij,bjk->bik', tri, oh, preferred_element_type=F32)
    tot = within[:, -1, :]
    before = jnp.cumsum(tot, axis=0) - tot
    ranks = (within + before[:, None, :]).reshape(NK, N_EXPERTS)
    rank = jnp.sum(jnp.where(onehot, ranks, 0.0), axis=1).astype(jnp.int32) - 1
    counts = (before[-1] + tot[-1]).astype(jnp.int32)
    padded = (counts + MOE_BLOCK - 1) // MOE_BLOCK * MOE_BLOCK
    pad_end = jnp.cumsum(padded)
    pad_start = pad_end - padded
    dest = pad_start[flat_e] + rank
    n_blocks = (NK + N_EXPERTS * (MOE_BLOCK - 1)) // MOE_BLOCK
    n_slots = n_blocks * MOE_BLOCK
    slot_tok = (jnp.arange(n_slots, dtype=jnp.int32) % N).at[dest].set(jnp.arange(NK, dtype=jnp.int32) // TOP_K)
    block_start = jnp.arange(n_blocks, dtype=jnp.int32) * MOE_BLOCK
    block_expert = jnp.minimum(jnp.sum((pad_end[None, :] <= block_start[:, None]).astype(jnp.int32), axis=1),
                               N_EXPERTS - 1)
    xs = h2[slot_tok]
    ys = _moe_experts(xs, block_expert, pad_end[-1] // MOE_BLOCK, w_gate, w_up, w_down, layer)
    dest2 = dest.reshape(N, TOP_K)
    return [ys[dest2[:, k]] for k in range(TOP_K)]


def _tile_mod(mod_rows, n_batch, tiles_per_batch, n_ctx_tiles):
    D = mod_rows.shape[1] // 6
    t = jnp.arange(n_batch * tiles_per_batch)
    src = jnp.where(t % tiles_per_batch < n_ctx_tiles, n_batch, t // tiles_per_batch)
    table = mod_rows.reshape(mod_rows.shape[0], 6, D)[src]
    return jnp.pad(table, ((0, 0), (0, 2), (0, 0)))


def kernel(x, c, ctx, c_ctx, mod_w, mod_b, norm_mix, norm_ffn, lru_w_in, lru_conv_w, lru_conv_b, lru_gate_a_w, lru_gate_a_b, lru_gate_x_w, lru_gate_x_b, lru_lambda, lru_w_out, dn_w_in, dn_conv_w, dn_a_log, dn_dt_bias, dn_norm, dn_w_out, ret_w_in, ret_norm, ret_w_out, att_w_in, att_q_norm, att_k_norm, att_sink, att_w_out, router_w, router_b, moe_w_gate, moe_w_up, moe_w_down):
    Bb, S, D = x.shape
    n_ctx = ctx.shape[1]
    TB = n_ctx + S
    M = Bb * TB
    tiles_per_batch = TB // ROW_TILE
    assert n_ctx % ROW_TILE == 0 and S % ROW_TILE == 0 and Bb < 16

    xa = jnp.concatenate([ctx, x], axis=1).reshape(M, D)
    cond = jnp.zeros((16, D), F32).at[:Bb].set(jax.nn.silu(c)).at[Bb].set(jax.nn.silu(c_ctx)).astype(BF16)
    mod_rows = _adaln_rows(cond, mod_w, mod_b)
    modtiles = [_tile_mod(mod_rows[i], Bb, tiles_per_batch, n_ctx // ROW_TILE) for i in range(DEPTH)]
    router_w_f32 = jnp.pad(router_w.astype(F32), ((0, 0), (0, LANES - N_EXPERTS)))
    router_w_hi = router_w_f32.astype(BF16)
    router_w_pad = jnp.stack([router_w_hi, (router_w_f32 - router_w_hi.astype(F32)).astype(BF16)])
    router_b_pad = jnp.pad(router_b.astype(F32), (0, LANES - N_EXPERTS))[None]
    rope_att = _rope_tables(n_ctx, S, ATT_HD)
    rope_ret = _rope_tables(n_ctx, S, RET_DK)

    def row(width, col=0, dtype_rows=ROW_TILE):
        return pl.BlockSpec((dtype_rows, width), lambda t: (t, col))

    def vec(width):
        return pl.BlockSpec((1, width), lambda t: (0, 0))

    tm_proj = M // 8 if M % (8 * 16) == 0 else 512
    h = _prenorm(xa, norm_mix[0][None], modtiles[0])
    for i in range(DEPTH):
        kind, j = i % N_MIXERS, i // N_MIXERS
        tm = ROW_TILE
        if kind == 0:
            p = _matmul(h, lru_w_in, layer=j, tm=tm_proj)
            cw = jnp.pad(lru_conv_w[j], ((0, SUBLANES - CONV_W), (0, 0)))
            dirs = [_rglru(p, cw, lru_conv_b[j][None], lru_gate_a_w[j, d].astype(BF16), lru_gate_a_b[j, d][None],
                           lru_gate_x_w[j, d].astype(BF16), lru_gate_x_b[j, d][None], lru_lambda[j, d][None],
                           Bb, n_ctx, S, reverse=(d == 1)) for d in range(2)]
            fin = (_finish_lru, [p] + dirs, [row(LRU_WIDTH, 0), row(LRU_WIDTH), row(LRU_WIDTH)])
            w_out = lru_w_out[j]
        elif kind == 1:
            HK = DN_HEADS * DN_DK
            p = _matmul(h, dn_w_in, layer=j, n_out=4 * HK, tm=tm_proj)
            tail = _matmul(h, jnp.pad(dn_w_in[j][:, 4 * HK:], ((0, 0), (0, LANES - 4 * DN_HEADS))), tm=tm_proj)
            cw = jnp.pad(dn_conv_w[j], ((0, SUBLANES - CONV_W), (0, 0)))
            alog_row = jnp.zeros((1, LANES), F32).at[0, 2 * DN_HEADS:4 * DN_HEADS].set(dn_a_log[j].reshape(-1))
            dtb_row = jnp.zeros((1, LANES), F32).at[0, 2 * DN_HEADS:4 * DN_HEADS].set(dn_dt_bias[j].reshape(-1))
            dirs = [_deltanet(p, tail, cw, alog_row, dtb_row, Bb, n_ctx, S, d) for d in range(2)]
            fin = (_finish_dn, [p] + dirs + [jnp.tile(dn_norm[j], DN_HEADS)[None]],
                   [row(HK, 3), row(HK), row(HK), vec(HK)])
            w_out = dn_w_out[j]
        elif kind == 2:
            tm = ROW_TILE // 2
            HV = RET_HEADS * RET_DV
            p = _matmul(h, ret_w_in, layer=j, tm=tm_proj)
            dirs = [_retention(p, rope_ret[0], rope_ret[1], Bb, n_ctx, S, reverse=(d == 1)) for d in range(2)]
            fin = (_finish_ret, [p] + dirs + [ret_norm[j][None]],
                   [row(HV, 2, tm), row(HV, 0, tm), row(HV, 0, tm), vec(HV)])
            w_out = ret_w_out[j]
        else:
            p = _matmul(h, att_w_in, layer=j, tm=tm_proj)
            q, k, v = _att_prep(p, att_q_norm[j][None], att_k_norm[j][None], rope_att[0], rope_att[1],
                                tiles_per_batch)
            o = _attention(q, k, v, att_sink[j], Bb, n_ctx, S)
            fin = (_finish_att, [o], [row(ATT_HEADS * ATT_HD)])
            w_out = att_w_out[j]

        xa, h2, route = _outproj(fin[0], fin[1], fin[2], w_out.astype(BF16), xa, norm_ffn[i][None],
                                 modtiles[i], router_w_pad, router_b_pad, tm)
        ys = _moe(h2, route, moe_w_gate, moe_w_up, moe_w_down, i)
        if i + 1 < DEPTH:
            xa, h = _resid_norm(xa, ys, route, norm_mix[i + 1][None], modtiles[i], modtiles[i + 1])
        else:
            xa = _resid(xa, ys, route, modtiles[i])
    return xa.reshape(Bb, TB, D)[:, n_ctx:]
```

```python
import functools
import math

import jax
import jax.numpy as jnp
from jax import lax
from jax.experimental import pallas as pl
from jax.experimental.pallas import tpu as pltpu

D_MODEL = 2048
DEPTH = 4
GRID_W = 64
N_MIXERS = 4
NORM_EPS = 1e-6
ROPE_BASE = 10000.0
CONV_W = 4

LRU_WIDTH = D_MODEL
LRU_BLOCKS = 8
LRU_BLOCK = LRU_WIDTH // LRU_BLOCKS
LRU_C = 8.0

DN_HEADS = 16
DN_DK = D_MODEL // DN_HEADS
DN_DV = D_MODEL // DN_HEADS

RET_HEADS = 8
RET_DK = D_MODEL // RET_HEADS
RET_DV = 2 * D_MODEL // RET_HEADS

ATT_HEADS = 16
ATT_KV_HEADS = 4
ATT_HD = D_MODEL // ATT_HEADS
ATT_G = ATT_HEADS // ATT_KV_HEADS
WINDOW = 128

N_EXPERTS = 32
N_GROUPS = 8
EXPERTS_PER_GROUP = N_EXPERTS // N_GROUPS
TOP_K = 2
D_EXPERT = 512
MOE_BLOCK = 256

F32 = jnp.float32
BF16 = jnp.bfloat16

VMEM_LIMIT_BYTES = 56 * 1024 * 1024
SUBLANES = 8
LANES = 128
ROW_TILE = 256
HALO = SUBLANES


def _params(*sem):
    return pltpu.CompilerParams(dimension_semantics=sem, vmem_limit_bytes=VMEM_LIMIT_BYTES)


def _sigmoid(x):
    return 0.5 * jnp.tanh(0.5 * x) + 0.5


def _silu(x):
    return x * _sigmoid(x)


def _softplus(x):
    return jnp.maximum(x, 0.0) + jnp.log(1.0 + jnp.exp(-jnp.abs(x)))


def _mm_kernel(x_ref, w_ref, o_ref, wb_ref):
    @pl.when(pl.program_id(1) == 0)
    def _():
        wb_ref[...] = w_ref[...].astype(BF16)

    o_ref[...] = jnp.dot(x_ref[...], wb_ref[...], preferred_element_type=F32)


def _matmul(x, w, layer=None, n_out=None, tm=512, tn=1024):
    M, K = x.shape
    N = w.shape[-1] if n_out is None else n_out
    tm = min(tm, M)
    tn = min(tn, N)
    assert M % tm == 0 and N % tn == 0, (M, N, tm, tn)
    if layer is None:
        w_spec = pl.BlockSpec((K, tn), lambda n, m: (0, n))
    else:
        w_spec = pl.BlockSpec((None, K, tn), lambda n, m: (layer, 0, n))
    return pl.pallas_call(
        _mm_kernel,
        grid=(N // tn, M // tm),
        in_specs=[pl.BlockSpec((tm, K), lambda n, m: (m, 0)), w_spec],
        out_specs=pl.BlockSpec((tm, tn), lambda n, m: (m, n)),
        out_shape=jax.ShapeDtypeStruct((M, N), F32),
        scratch_shapes=[pltpu.VMEM((K, tn), BF16)],
        compiler_params=_params("arbitrary", "arbitrary"),
        name="dense_matmul",
    )(x, w)


def _mod_kernel(c_ref, w_ref, b_ref, o_ref):
    o_ref[...] = jnp.dot(c_ref[...], w_ref[...].astype(BF16), preferred_element_type=F32) + b_ref[...]


def _adaln_rows(cond, mod_w, mod_b, tn=2048):
    R, D = cond.shape
    L, _, N = mod_w.shape
    return pl.pallas_call(
        _mod_kernel,
        grid=(L, N // tn),
        in_specs=[pl.BlockSpec((R, D), lambda l, n: (0, 0)),
                  pl.BlockSpec((None, D, tn), lambda l, n: (l, 0, n)),
                  pl.BlockSpec((None, 1, tn), lambda l, n: (l, 0, n))],
        out_specs=pl.BlockSpec((None, R, tn), lambda l, n: (l, 0, n)),
        out_shape=jax.ShapeDtypeStruct((L, R, N), F32),
        compiler_params=_params("arbitrary", "arbitrary"),
        name="adaln_rows",
    )(cond, mod_w, mod_b.reshape(L, 1, N))


def _norm_mod(x, normw, shift, scale):
    y = x * lax.rsqrt(jnp.mean(x * x, axis=-1, keepdims=True) + NORM_EPS)
    return (y * normw) * (1.0 + scale) + shift


def _prenorm_kernel(tiles_per_batch, n_ctx_tiles, ctx_ref, x_ref, nw_ref, mod_ref, xa_ref, h_ref):
    is_ctx = (pl.program_id(0) % tiles_per_batch) < n_ctx_tiles
    xa = jnp.where(is_ctx, ctx_ref[...], x_ref[...])
    xa_ref[...] = xa
    h_ref[...] = _norm_mod(xa, nw_ref[...], mod_ref[0:1, :], mod_ref[1:2, :]).astype(BF16)


def _prenorm(ctx, x, normw, modtile):
    Bb, n_ctx, D = ctx.shape
    S = x.shape[1]
    nct, nxt = n_ctx // ROW_TILE, S // ROW_TILE
    tpb = nct + nxt
    M = Bb * (n_ctx + S)
    row = pl.BlockSpec((ROW_TILE, D), lambda t: (t, 0))
    return pl.pallas_call(
        functools.partial(_prenorm_kernel, tpb, nct),
        grid=(Bb * tpb,),
        in_specs=[pl.BlockSpec((None, ROW_TILE, D), lambda t: (t // tpb, jnp.minimum(t % tpb, nct - 1), 0)),
                  pl.BlockSpec((None, ROW_TILE, D), lambda t: (t // tpb, jnp.maximum(t % tpb - nct, 0), 0)),
                  pl.BlockSpec((1, D), lambda t: (0, 0)),
                  pl.BlockSpec((None, 8, D), lambda t: (t, 0, 0))],
        out_specs=[row, row],
        out_shape=[jax.ShapeDtypeStruct((M, D), F32), jax.ShapeDtypeStruct((M, D), BF16)],
        compiler_params=_params("arbitrary"),
        name="prenorm",
    )(ctx, x, normw, modtile)


def _ffn_residual(x_ref, y_refs, rt_ref, mod_ref):
    f = None
    for k, y_ref in enumerate(y_refs):
        lane = TOP_K + k
        term = y_ref[...] * rt_ref[:, lane:lane + 1]
        f = term if f is None else f + term
    return x_ref[...] + mod_ref[5:6, :] * f


def _resid_norm_kernel(x_ref, y0_ref, y1_ref, rt_ref, nw_ref, mod_ref, modn_ref, xo_ref, h_ref):
    x = _ffn_residual(x_ref, (y0_ref, y1_ref), rt_ref, mod_ref)
    xo_ref[...] = x
    h_ref[...] = _norm_mod(x, nw_ref[...], modn_ref[0:1, :], modn_ref[1:2, :]).astype(BF16)


def _resid_norm(x, ys, route, normw_next, modtile, modtile_next):
    M, D = x.shape
    row = pl.BlockSpec((ROW_TILE, D), lambda t: (t, 0))
    mod = pl.BlockSpec((None, 8, D), lambda t: (t, 0, 0))
    return pl.pallas_call(
        _resid_norm_kernel,
        grid=(M // ROW_TILE,),
        in_specs=[row, row, row, pl.BlockSpec((ROW_TILE, LANES), lambda t: (t, 0)),
                  pl.BlockSpec((1, D), lambda t: (0, 0)), mod, mod],
        out_specs=[row, row],
        out_shape=[jax.ShapeDtypeStruct((M, D), F32), jax.ShapeDtypeStruct((M, D), BF16)],
        compiler_params=_params("arbitrary"),
        name="resid_norm",
    )(x, ys[0], ys[1], route, normw_next, modtile, modtile_next)


def _resid_kernel(x_ref, y0_ref, y1_ref, rt_ref, mod_ref, xo_ref):
    xo_ref[...] = _ffn_residual(x_ref, (y0_ref, y1_ref), rt_ref, mod_ref)


def _resid_latent(x, ys, route, modtile, n_batch, n_ctx, seq):
    M, D = x.shape
    nct, nxt = n_ctx // ROW_TILE, seq // ROW_TILE
    tpb = nct + nxt
    row = pl.BlockSpec((ROW_TILE, D), lambda t: (t, 0))
    out = pl.BlockSpec((ROW_TILE, D), lambda t: ((t // tpb) * nxt + jnp.maximum(t % tpb - nct, 0), 0))
    return pl.pallas_call(
        _resid_kernel,
        grid=(M // ROW_TILE,),
        in_specs=[row, row, row, pl.BlockSpec((ROW_TILE, LANES), lambda t: (t, 0)),
                  pl.BlockSpec((None, 8, D), lambda t: (t, 0, 0))],
        out_specs=out,
        out_shape=jax.ShapeDtypeStruct((n_batch * seq, D), F32),
        compiler_params=_params("arbitrary"),
        name="resid",
    )(x, ys[0], ys[1], route, modtile)


def _head_rms(o, width, normw):
    parts = []
    for h in range(o.shape[1] // width):
        oh = o[:, h * width:(h + 1) * width]
        parts.append(oh * lax.rsqrt(jnp.mean(oh * oh, axis=-1, keepdims=True) + NORM_EPS))
    return jnp.concatenate(parts, axis=1) * normw


def _both_directions(f_ref, b_ref):
    return f_ref[...].astype(F32) + b_ref[...].astype(F32)


def _finish_lru(gate_ref, hf_ref, hb_ref):
    return jax.nn.gelu(gate_ref[...]) * _both_directions(hf_ref, hb_ref)


def _finish_dn(z_ref, of_ref, ob_ref, nw_ref):
    return _head_rms(_both_directions(of_ref, ob_ref), DN_DV, nw_ref[...]) * _silu(z_ref[...])


def _finish_ret(gate_ref, of_ref, ob_ref, nw_ref):
    return _head_rms(_both_directions(of_ref, ob_ref), RET_DV, nw_ref[...]) * _silu(gate_ref[...])


def _finish_att(o_ref):
    return o_ref[...]


ROUTE_WEIGHT_LANE = TOP_K


def _route_tile(logits, bias):
    assert TOP_K == 2 and EXPERTS_PER_GROUP == 4
    lane = lax.broadcasted_iota(jnp.int32, logits.shape, 1)
    valid = lane < N_EXPERTS
    neg = -jnp.inf
    scores = _sigmoid(logits)
    b = jnp.where(valid, scores + bias, neg)
    j = lane % EXPERTS_PER_GROUP
    jf = j.astype(F32)
    gf = (lane // EXPERTS_PER_GROUP).astype(F32)

    def group_reduce(v, op):
        v = op(v, jnp.where(j % 2 == 0, pltpu.roll(v, LANES - 1, axis=1), pltpu.roll(v, 1, axis=1)))
        return op(v, jnp.where(j < 2, pltpu.roll(v, LANES - 2, axis=1), pltpu.roll(v, 2, axis=1)))

    m1 = group_reduce(b, jnp.maximum)
    is1 = jf == group_reduce(jnp.where(b == m1, jf, float(EXPERTS_PER_GROUP)), jnp.minimum)
    b2 = jnp.where(is1, neg, b)
    m2 = group_reduce(b2, jnp.maximum)
    is2 = jf == group_reduce(jnp.where(b2 == m2, jf, float(EXPERTS_PER_GROUP)), jnp.minimum)
    gscore = jnp.where(valid, m1 + m2, neg)
    best = jnp.max(gscore, axis=-1, keepdims=True)
    gsel = jnp.min(jnp.where(gscore == best, gf, float(N_GROUPS)), axis=-1, keepdims=True)
    in_group = valid & (gf == gsel)
    lf = lane.astype(F32)

    def pick(mask, v):
        return jnp.sum(jnp.where(in_group & mask, v, 0.0), axis=-1, keepdims=True)

    e1, e2 = pick(is1, lf), pick(is2, lf)
    s1, s2 = pick(is1, scores), pick(is2, scores)
    tot = s1 + s2
    return jnp.where(lane == 0, e1, jnp.where(lane == 1, e2, jnp.where(
        lane == ROUTE_WEIGHT_LANE, s1 / tot, jnp.where(lane == ROUTE_WEIGHT_LANE + 1, s2 / tot, 0.0))))


def _outproj_kernel(finish, n_fin, *refs):
    fin_refs = refs[:n_fin]
    w_ref, x_ref, nw_ref, mod_ref, rw_ref, rb_ref, xo_ref, h_ref, rt_ref = refs[n_fin:]
    a = finish(*fin_refs).astype(BF16)
    y = jnp.dot(a, w_ref[...], preferred_element_type=F32)
    x = x_ref[...] + mod_ref[2:3, :] * y
    xo_ref[...] = x
    h = _norm_mod(x, nw_ref[...], mod_ref[3:4, :], mod_ref[4:5, :])
    h_ref[...] = h
    h_hi = h.astype(BF16)
    h_lo = (h - h_hi.astype(F32)).astype(BF16)
    logits = (jnp.dot(h_hi, rw_ref[0], preferred_element_type=F32)
              + jnp.dot(h_lo, rw_ref[0], preferred_element_type=F32)
              + jnp.dot(h_hi, rw_ref[1], preferred_element_type=F32))
    rt_ref[...] = _route_tile(logits, rb_ref[...])


def _outproj(finish, fin_args, fin_specs, w_bf16, x, normw_ffn, modtile, router_w_pad, router_b_pad, tm):
    M, D = x.shape
    K = w_bf16.shape[0]
    per = ROW_TILE // tm
    row = pl.BlockSpec((tm, D), lambda t: (t, 0))
    in_specs = list(fin_specs) + [
        pl.BlockSpec((K, D), lambda t: (0, 0)),
        row,
        pl.BlockSpec((1, D), lambda t: (0, 0)),
        pl.BlockSpec((None, 8, D), lambda t: (t // per, 0, 0)),
        pl.BlockSpec((2, D, LANES), lambda t: (0, 0, 0)),
        pl.BlockSpec((1, LANES), lambda t: (0, 0)),
    ]
    return pl.pallas_call(
        functools.partial(_outproj_kernel, finish, len(fin_args)),
        grid=(M // tm,),
        in_specs=in_specs,
        out_specs=[row, row, pl.BlockSpec((tm, LANES), lambda t: (t, 0))],
        out_shape=[jax.ShapeDtypeStruct((M, D), F32), jax.ShapeDtypeStruct((M, D), F32),
                   jax.ShapeDtypeStruct((M, LANES), F32)],
        compiler_params=_params("arbitrary"),
        name="outproj",
    )(*fin_args, w_bf16, x, normw_ffn, modtile, router_w_pad, router_b_pad)


def _rope_tables(n_ctx, seq, head_dim):
    quarter = head_dim // 4
    pos = jnp.arange(seq)
    inv_freq = ROPE_BASE ** (-jnp.arange(quarter, dtype=F32) / quarter)
    ang_r = (pos // GRID_W).astype(F32)[:, None] * inv_freq
    ang_c = (pos % GRID_W).astype(F32)[:, None] * inv_freq
    c = jnp.concatenate([jnp.cos(ang_r), jnp.cos(ang_r), jnp.cos(ang_c), jnp.cos(ang_c)], axis=1)
    s = jnp.concatenate([-jnp.sin(ang_r), jnp.sin(ang_r), -jnp.sin(ang_c), jnp.sin(ang_c)], axis=1)
    c = jnp.concatenate([jnp.ones((n_ctx, head_dim), F32), c], axis=0)
    s = jnp.concatenate([jnp.zeros((n_ctx, head_dim), F32), s], axis=0)
    return c, s


def _rope(x, c, s, quarter):
    width = x.shape[1]
    if 2 * quarter == LANES:
        parts = [pltpu.roll(x[:, j:j + LANES], quarter, axis=1) for j in range(0, width, LANES)]
        partner = parts[0] if len(parts) == 1 else jnp.concatenate(parts, axis=1)
    else:
        assert width == LANES and 4 * quarter == LANES
        lane = lax.broadcasted_iota(jnp.int32, x.shape, 1)
        partner = jnp.where((lane % (2 * quarter)) < quarter,
                            pltpu.roll(x, LANES - quarter, axis=1), pltpu.roll(x, quarter, axis=1))
    return x * c + partner * s


ATT_TILE = 128


def _att_prep_kernel(p_ref, qn_ref, kn_ref, c_ref, s_ref, q_ref, k_ref, v_ref):
    c = c_ref[...]
    s = s_ref[...]
    qw = ATT_HEADS * ATT_HD
    kw = ATT_KV_HEADS * ATT_HD

    def norm_rope(xh, w):
        y = xh * lax.rsqrt(jnp.mean(xh * xh, axis=-1, keepdims=True) + NORM_EPS) * w
        return _rope(y, c, s, ATT_HD // 4)

    for h in range(ATT_HEADS):
        qh = norm_rope(p_ref[:, h * ATT_HD:(h + 1) * ATT_HD], qn_ref[...])
        q_ref[:, h * ATT_HD:(h + 1) * ATT_HD] = (qh * ATT_HD ** -0.5).astype(BF16)
    for h in range(ATT_KV_HEADS):
        kh = norm_rope(p_ref[:, qw + h * ATT_HD:qw + (h + 1) * ATT_HD], kn_ref[...])
        k_ref[:, h * ATT_HD:(h + 1) * ATT_HD] = kh.astype(BF16)
    v_ref[...] = p_ref[:, qw + kw:qw + 2 * kw].astype(BF16)


def _att_prep(p, q_norm, k_norm, rope_c, rope_s, tiles_per_batch):
    M = p.shape[0]
    qw, kw = ATT_HEADS * ATT_HD, ATT_KV_HEADS * ATT_HD
    tab = pl.BlockSpec((ROW_TILE, ATT_HD), lambda t: (t % tiles_per_batch, 0))
    return pl.pallas_call(
        _att_prep_kernel,
        grid=(M // ROW_TILE,),
        in_specs=[pl.BlockSpec((ROW_TILE, qw + 2 * kw), lambda t: (t, 0)),
                  pl.BlockSpec((1, ATT_HD), lambda t: (0, 0)),
                  pl.BlockSpec((1, ATT_HD), lambda t: (0, 0)), tab, tab],
        out_specs=[pl.BlockSpec((ROW_TILE, qw), lambda t: (t, 0)),
                   pl.BlockSpec((ROW_TILE, kw), lambda t: (t, 0)),
                   pl.BlockSpec((ROW_TILE, kw), lambda t: (t, 0))],
        out_shape=[jax.ShapeDtypeStruct((M, qw), BF16), jax.ShapeDtypeStruct((M, kw), BF16),
                   jax.ShapeDtypeStruct((M, kw), BF16)],
        compiler_params=_params("arbitrary"),
        name="att_prep",
    )(p, q_norm, k_norm, rope_c, rope_s)


def _att_kernel(n_ctx_tiles, n_tiles, q_ref, kp_ref, ko_ref, kn_ref, vp_ref, vo_ref, vn_ref,
                kc_ref, vc_ref, sink_ref, o_ref):
    t = pl.program_id(1)
    T = ATT_TILE
    rows = ATT_G * T
    row = lax.broadcasted_iota(jnp.int32, (rows, 3 * T), 0) % T
    col = lax.broadcasted_iota(jnp.int32, (rows, 3 * T), 1)
    blk = col // T
    c = col % T
    latent = t >= n_ctx_tiles
    prev_ok = latent & (t >= n_ctx_tiles + 1)
    next_ok = latent & (t <= n_tiles - 2)
    band = ((blk == 1) & latent) | ((blk == 0) & (c >= row) & prev_ok) | ((blk == 2) & (c <= row) & next_ok)
    nt = (((1,), (1,)), ((), ()))
    KV = range(ATT_KV_HEADS)
    ks = [slice(kvh * ATT_HD, (kvh + 1) * ATT_HD) for kvh in KV]
    q4 = [jnp.concatenate([q_ref[:, (kvh * ATT_G + g) * ATT_HD:(kvh * ATT_G + g + 1) * ATT_HD]
                           for g in range(ATT_G)], axis=0) for kvh in KV]
    s_loc = [lax.dot_general(q4[i], jnp.concatenate([kp_ref[:, ks[i]], ko_ref[:, ks[i]], kn_ref[:, ks[i]]], axis=0),
                             nt, preferred_element_type=F32) for i in KV]
    s_ctx = [lax.dot_general(q4[i], kc_ref[:, ks[i]], nt, preferred_element_type=F32) for i in KV]
    s_loc = [jnp.where(band, s_loc[i], -jnp.inf) for i in KV]
    sink = [sink_ref[i][:, 0:1] for i in KV]
    m = [jnp.maximum(sink[i], jnp.maximum(jnp.max(s_loc[i], axis=-1, keepdims=True),
                                          jnp.max(s_ctx[i], axis=-1, keepdims=True))) for i in KV]
    p_loc = [jnp.exp(s_loc[i] - m[i]) for i in KV]
    p_ctx = [jnp.exp(s_ctx[i] - m[i]) for i in KV]
    den = [jnp.exp(sink[i] - m[i]) + jnp.sum(p_loc[i], axis=-1, keepdims=True)
           + jnp.sum(p_ctx[i], axis=-1, keepdims=True) for i in KV]
    o = [jnp.dot(p_loc[i].astype(BF16),
                 jnp.concatenate([vp_ref[:, ks[i]], vo_ref[:, ks[i]], vn_ref[:, ks[i]]], axis=0),
                 preferred_element_type=F32)
         + jnp.dot(p_ctx[i].astype(BF16), vc_ref[:, ks[i]], preferred_element_type=F32) for i in KV]
    for i in KV:
        oi = o[i] / den[i]
        for g in range(ATT_G):
            h = i * ATT_G + g
            o_ref[:, h * ATT_HD:(h + 1) * ATT_HD] = oi[g * T:(g + 1) * T, :].astype(BF16)


def _attention(q, k, v, sink, n_batch, n_ctx, seq):
    M, qw = q.shape
    kw = k.shape[1]
    T = ATT_TILE
    n_tiles = (n_ctx + seq) // T
    n_ctx_tiles = n_ctx // T
    sink_rows = jnp.broadcast_to(
        jnp.repeat(sink.astype(F32).reshape(ATT_KV_HEADS, ATT_G), T, axis=1)[:, :, None],
        (ATT_KV_HEADS, ATT_G * T, LANES))

    def tile(off):
        return pl.BlockSpec((T, kw), lambda b, t: (b * n_tiles + jnp.clip(t + off, 0, n_tiles - 1), 0))

    ctx = pl.BlockSpec((n_ctx, kw), lambda b, t: (b * (n_tiles // n_ctx_tiles), 0))
    return pl.pallas_call(
        functools.partial(_att_kernel, n_ctx_tiles, n_tiles),
        grid=(n_batch, n_tiles),
        in_specs=[pl.BlockSpec((T, qw), lambda b, t: (b * n_tiles + t, 0)),
                  tile(-1), tile(0), tile(1), tile(-1), tile(0), tile(1), ctx, ctx,
                  pl.BlockSpec((ATT_KV_HEADS, ATT_G * T, LANES), lambda b, t: (0, 0, 0))],
        out_specs=pl.BlockSpec((T, qw), lambda b, t: (b * n_tiles + t, 0)),
        out_shape=jax.ShapeDtypeStruct((M, qw), BF16),
        compiler_params=_params("arbitrary", "arbitrary"),
        name="window_attention",
    )(q, k, k, k, v, v, v, k, v, sink_rows)


def _chunk_of_step(s, n_chunks, n_ctx_chunks, reverse):
    if not reverse:
        return s
    return jnp.where(s < n_ctx_chunks, n_ctx_chunks - 1 - s, n_chunks - 1 - (s - n_ctx_chunks))


RET_CHUNK = ROW_TILE


def _ret_kernel(reverse, q_ref, k_ref, v_ref, c_ref, s_ref, o_ref, state_ref):
    @pl.when(pl.program_id(1) == 0)
    def _():
        state_ref[...] = jnp.zeros_like(state_ref)

    C = RET_CHUNK
    c = c_ref[...]
    s = s_ref[...]
    i = lax.broadcasted_iota(jnp.int32, (C, C), 0)
    j = lax.broadcasted_iota(jnp.int32, (C, C), 1)
    pos = lax.broadcasted_iota(jnp.int32, (C, 1), 0).astype(F32)
    diff = (j - i) if reverse else (i - j)
    mask = diff > 0 if reverse else diff >= 0
    dist = jnp.maximum(diff, 0).astype(F32)
    steps_in = (C - pos) if reverse else (pos + 1.0)
    steps_out = pos if reverse else (C - 1.0 - pos)
    nt = (((1,), (1,)), ((), ()))
    for h in range(RET_HEADS):
        log_gamma = math.log1p(-2.0 ** (-5.0 - h))
        q = _rope(q_ref[:, h * RET_DK:(h + 1) * RET_DK], c, s, RET_DK // 4)
        k = _rope(k_ref[:, h * RET_DK:(h + 1) * RET_DK], c, s, RET_DK // 4) * RET_DK ** -0.5
        v = v_ref[:, h * RET_DV:(h + 1) * RET_DV].astype(BF16)
        qb = q.astype(BF16)
        dmat = jnp.where(mask, jnp.exp(dist * log_gamma), 0.0)
        a = lax.dot_general(qb, k.astype(BF16), nt, preferred_element_type=F32) * dmat
        st = state_ref[h]
        o = jnp.dot(a.astype(BF16), v, preferred_element_type=F32)
        o = o + jnp.dot(qb, st.astype(BF16), preferred_element_type=F32) * jnp.exp(steps_in * log_gamma)
        o_ref[:, h * RET_DV:(h + 1) * RET_DV] = o.astype(BF16)
        kt = (k * jnp.exp(steps_out * log_gamma)).T.astype(BF16)
        state_ref[h] = st * math.exp(C * log_gamma) + jnp.dot(kt, v, preferred_element_type=F32)


def _retention(p, rope_c, rope_s, n_batch, n_ctx, seq, reverse):
    M = p.shape[0]
    C = RET_CHUNK
    n_chunks = (n_ctx + seq) // C
    n_ctx_chunks = n_ctx // C
    hk, hv = RET_HEADS * RET_DK, RET_HEADS * RET_DV

    def chunk(b, s):
        return _chunk_of_step(s, n_chunks, n_ctx_chunks, reverse)

    tab = pl.BlockSpec((C, RET_DK), lambda b, s: (chunk(b, s), 0))
    return pl.pallas_call(
        functools.partial(_ret_kernel, reverse),
        grid=(n_batch, n_chunks),
        in_specs=[pl.BlockSpec((C, hk), lambda b, s: (b * n_chunks + chunk(b, s), 0)),
                  pl.BlockSpec((C, hk), lambda b, s: (b * n_chunks + chunk(b, s), 1)),
                  pl.BlockSpec((C, hv), lambda b, s: (b * n_chunks + chunk(b, s), 1)),
                  tab, tab],
        out_specs=pl.BlockSpec((C, hv), lambda b, s: (b * n_chunks + chunk(b, s), 0)),
        out_shape=jax.ShapeDtypeStruct((M, hv), BF16),
        scratch_shapes=[pltpu.VMEM((RET_HEADS, RET_DK, RET_DV), F32)],
        compiler_params=_params("arbitrary", "arbitrary"),
        name="retention_rev" if reverse else "retention_fwd",
    )(p, p, p, rope_c, rope_s)


LRU_CHUNK = ROW_TILE


def _lru_kernel(reverse, n_chunks, n_ctx_chunks, x_ref, xp_ref, xn_ref, cw_ref, cb_ref, wa_ref, ba_ref,
                wx_ref, bx_ref, lam_ref, o_ref, h_ref):
    step = pl.program_id(1)

    @pl.when(step == 0)
    def _():
        h_ref[...] = jnp.zeros_like(h_ref)

    C = LRU_CHUNK
    W = LRU_BLOCK
    chunk = _chunk_of_step(step, n_chunks, n_ctx_chunks, reverse)
    has_prev = ((chunk != 0) & (chunk != n_ctx_chunks)).astype(F32)
    has_next = ((chunk != n_ctx_chunks - 1) & (chunk != n_chunks - 1)).astype(F32)
    rows = C + 2 * HALO
    sub = lax.broadcasted_iota(jnp.int32, (C, W), 0) % SUBLANES

    def block(n, carry):
        lanes = pl.ds(pl.multiple_of(n * W, W), W)
        full = jnp.concatenate([xp_ref[:, lanes] * has_prev, x_ref[:, lanes], xn_ref[:, lanes] * has_next], axis=0)
        cw = cw_ref[:, lanes]
        xc = (cw[0:1] * pltpu.roll(full, 2, axis=0)[HALO:HALO + C]
              + cw[1:2] * pltpu.roll(full, 1, axis=0)[HALO:HALO + C]
              + cw[2:3] * full[HALO:HALO + C]
              + cw[3:4] * pltpu.roll(full, rows - 1, axis=0)[HALO:HALO + C]) + cb_ref[:, lanes]
        xb = xc.astype(BF16)
        r = _sigmoid(jnp.dot(xb, wa_ref[n], preferred_element_type=F32) + ba_ref[:, lanes])
        gi = _sigmoid(jnp.dot(xb, wx_ref[n], preferred_element_type=F32) + bx_ref[:, lanes])
        a = jnp.exp((-LRU_C) * r * _softplus(-lam_ref[:, lanes]))
        b = jnp.sqrt(jnp.maximum(1.0 - a * a, 0.0)) * (gi * xc)
        for k in (1, 2, 4):
            if reverse:
                keep = sub < SUBLANES - k
                a_sh = pltpu.roll(a, C - k, axis=0)
                b_sh = pltpu.roll(b, C - k, axis=0)
            else:
                keep = sub >= k
                a_sh = pltpu.roll(a, k, axis=0)
                b_sh = pltpu.roll(b, k, axis=0)
            b = b + a * jnp.where(keep, b_sh, 0.0)
            a = a * jnp.where(keep, a_sh, 1.0)
        hin = h_ref[0:1, lanes]
        groups = C // SUBLANES
        outs = [None] * groups
        for g in (range(groups - 1, -1, -1) if reverse else range(groups)):
            lo = g * SUBLANES
            hg = b[lo:lo + SUBLANES] + a[lo:lo + SUBLANES] * hin
            outs[g] = hg
            hin = hg[0:1] if reverse else hg[SUBLANES - 1:SUBLANES]
        o_ref[:, lanes] = jnp.concatenate(outs, axis=0).astype(BF16)
        h_ref[:, lanes] = jnp.broadcast_to(hin, (SUBLANES, W))
        return carry

    lax.fori_loop(0, LRU_BLOCKS, block, 0)


def _rglru(p, conv_w, conv_b, gate_a_w, gate_a_b, gate_x_w, gate_x_b, lam, n_batch, n_ctx, seq, reverse):
    M = p.shape[0]
    C = LRU_CHUNK
    Wd = LRU_WIDTH
    n_chunks = (n_ctx + seq) // C
    n_ctx_chunks = n_ctx // C
    per = C // HALO
    last = M // HALO - 1

    def tile(b, s):
        return b * n_chunks + _chunk_of_step(s, n_chunks, n_ctx_chunks, reverse)

    vec = pl.BlockSpec((1, Wd), lambda b, s: (0, 0))
    wts = pl.BlockSpec((LRU_BLOCKS, LRU_BLOCK, LRU_BLOCK), lambda b, s: (0, 0, 0))
    return pl.pallas_call(
        functools.partial(_lru_kernel, reverse, n_chunks, n_ctx_chunks),
        grid=(n_batch, n_chunks),
        in_specs=[pl.BlockSpec((C, Wd), lambda b, s: (tile(b, s), 1)),
                  pl.BlockSpec((HALO, Wd), lambda b, s: (jnp.maximum(tile(b, s) * per - 1, 0), 1)),
                  pl.BlockSpec((HALO, Wd), lambda b, s: (jnp.minimum((tile(b, s) + 1) * per, last), 1)),
                  pl.BlockSpec((SUBLANES, Wd), lambda b, s: (0, 0)),
                  vec, wts, vec, wts, vec, vec],
        out_specs=pl.BlockSpec((C, Wd), lambda b, s: (tile(b, s), 0)),
        out_shape=jax.ShapeDtypeStruct((M, Wd), BF16),
        scratch_shapes=[pltpu.VMEM((SUBLANES, Wd), F32)],
        compiler_params=_params("arbitrary", "arbitrary"),
        name="rglru_rev" if reverse else "rglru_fwd",
    )(p, p, p, conv_w, conv_b, gate_a_w, gate_a_b, gate_x_w, gate_x_b, lam)


DN_CHUNK = 128
DN_HEAD_GROUP = 8
DN_INV_BASE = SUBLANES
DN_NEUMANN_ROUNDS = 2


def _dn_kernel(reverse, d, n_chunks, n_ctx_chunks, q_ref, k_ref, v_ref, xp_ref, xn_ref, tail_ref, cw_ref,
               alog_ref, dtb_ref, o_ref, state_ref, gct_ref, bt_ref):
    step = pl.program_id(1)

    @pl.when(step == 0)
    def _():
        state_ref[...] = jnp.zeros_like(state_ref)

    C = DN_CHUNK
    H = DN_HEADS
    HK = H * DN_DK
    chunk = _chunk_of_step(step, n_chunks, n_ctx_chunks, reverse)
    has_prev = ((chunk != 0) & (chunk != n_ctx_chunks)).astype(F32)
    has_next = ((chunk != n_ctx_chunks - 1) & (chunk != n_chunks - 1)).astype(F32)
    rows = C + 2 * HALO
    ci = lax.broadcasted_iota(jnp.int32, (C, C), 0)
    si = lax.broadcasted_iota(jnp.int32, (C, C), 1)
    mask = (si >= ci) if reverse else (si <= ci)
    strict = (si > ci) if reverse else (si < ci)
    eye = (si == ci).astype(F32)
    last = 0 if reverse else C - 1
    same_base = (ci // DN_INV_BASE) == (si // DN_INV_BASE)
    merge_masks = []
    size = DN_INV_BASE
    while size < C:
        cb, sb = ci // size, si // size
        merge_masks.append(((cb % 2 == 0) & (sb == cb + 1)) if reverse else ((cb % 2 == 1) & (sb == cb - 1)))
        size *= 2

    tail = tail_ref[...]
    beta = _sigmoid(tail)
    g = -jnp.exp(alog_ref[...]) * _softplus(tail + dtb_ref[...])
    gcum = jnp.dot(mask.astype(F32), g, preferred_element_type=F32, precision=lax.Precision.HIGHEST)
    gct_ref[...] = gcum.T
    bt_ref[...] = beta.T
    nt = (((1,), (1,)), ((), ()))

    def conv_silu(ref, ref_off, h):
        lanes_in = pl.ds(pl.multiple_of(h * DN_DK, DN_DK), DN_DK)
        lanes_all = pl.ds(pl.multiple_of(ref_off + h * DN_DK, DN_DK), DN_DK)
        full = jnp.concatenate([xp_ref[:, lanes_all] * has_prev, ref[:, lanes_in], xn_ref[:, lanes_all] * has_next],
                               axis=0)
        cw = cw_ref[:, lanes_all]
        y = (cw[0:1] * pltpu.roll(full, 2, axis=0)[HALO:HALO + C]
             + cw[1:2] * pltpu.roll(full, 1, axis=0)[HALO:HALO + C]
             + cw[2:3] * full[HALO:HALO + C]
             + cw[3:4] * pltpu.roll(full, rows - 1, axis=0)[HALO:HALO + C])
        return _silu(y)

    def l2n(x):
        return x * lax.rsqrt(jnp.sum(x * x, axis=-1, keepdims=True) + NORM_EPS)

    def dot(a, b):
        return jnp.dot(a.astype(BF16), b.astype(BF16), preferred_element_type=F32)

    def dot_nt(a, b):
        return lax.dot_general(a.astype(BF16), b.astype(BF16), nt, preferred_element_type=F32)

    def group(gi, carry):
        hs = [gi * DN_HEAD_GROUP + jj for jj in range(DN_HEAD_GROUP)]
        G = range(DN_HEAD_GROUP)
        sts = [state_ref[h] for h in hs]
        g_row = [gct_ref[pl.ds(2 * H + d * H + h, 1), :] for h in hs]
        g_rows = [jnp.broadcast_to(g_row[i], (C, C)) for i in G]
        g_cols = [g_rows[i].T for i in G]
        b_cols = [jnp.broadcast_to(bt_ref[pl.ds(d * H + h, 1), :], (C, C)).T for h in hs]
        g_last = [g_row[i][:, last:last + 1] for i in G]
        decay = [jnp.where(mask, jnp.exp(jnp.where(mask, g_cols[i] - g_rows[i], 0.0)), 0.0) for i in G]
        q = [l2n(conv_silu(q_ref, 0, h)) * DN_DK ** -0.5 for h in hs]
        k = [l2n(conv_silu(k_ref, HK, h)) for h in hs]
        v = [conv_silu(v_ref, 2 * HK, h) for h in hs]
        kb = [k[i] * b_cols[i] for i in G]
        m = [jnp.where(strict, dot_nt(kb[i], k[i]) * decay[i], 0.0) for i in G]
        a_intra = [dot_nt(q[i], k[i]) * decay[i] for i in G]

        n = [jnp.where(same_base, -m[i], 0.0) for i in G]
        t = [eye + n[i] for i in G]
        for _ in range(DN_NEUMANN_ROUNDS):
            n = [dot(n[i], n[i]) for i in G]
            t = [t[i] + dot(t[i], n[i]) for i in G]
        for pair in merge_masks:
            x = [dot(t[i], jnp.where(pair, m[i], 0.0)) for i in G]
            t = [t[i] - dot(x[i], t[i]) for i in G]
        sol = [dot(t[i], jnp.concatenate([v[i] * b_cols[i], kb[i] * jnp.exp(g_cols[i])], axis=1)) for i in G]

        r = [dot(jnp.concatenate([sol[i][:, DN_DV:], q[i] * jnp.exp(g_cols[i])], axis=0), sts[i]) for i in G]
        v_new = [sol[i][:, :DN_DV] - r[i][:C] for i in G]
        o = [r[i][C:] + dot(a_intra[i], v_new[i]) for i in G]
        st_new = [sts[i] * jnp.exp(g_last[i]) + dot((k[i] * jnp.exp(g_last[i] - g_cols[i])).T, v_new[i]) for i in G]
        for i, h in enumerate(hs):
            o_ref[:, pl.ds(pl.multiple_of(h * DN_DV, DN_DV), DN_DV)] = o[i].astype(BF16)
            state_ref[h] = st_new[i]
        return carry

    lax.fori_loop(0, H // DN_HEAD_GROUP, group, 0)


def _deltanet(p, tail, conv_w, alog_row, dtb_row, n_batch, n_ctx, seq, d):
    M = p.shape[0]
    C = DN_CHUNK
    reverse = d == 1
    n_chunks = (n_ctx + seq) // C
    n_ctx_chunks = n_ctx // C
    HK = DN_HEADS * DN_DK
    per = C // HALO
    last = M // HALO - 1

    def tile(b, s):
        return b * n_chunks + _chunk_of_step(s, n_chunks, n_ctx_chunks, reverse)

    row1 = pl.BlockSpec((1, LANES), lambda b, s: (0, 0))
    return pl.pallas_call(
        functools.partial(_dn_kernel, reverse, d, n_chunks, n_ctx_chunks),
        grid=(n_batch, n_chunks),
        in_specs=[pl.BlockSpec((C, HK), lambda b, s: (tile(b, s), 0)),
                  pl.BlockSpec((C, HK), lambda b, s: (tile(b, s), 1)),
                  pl.BlockSpec((C, HK), lambda b, s: (tile(b, s), 2)),
                  pl.BlockSpec((HALO, 4 * HK), lambda b, s: (jnp.maximum(tile(b, s) * per - 1, 0), 0)),
                  pl.BlockSpec((HALO, 4 * HK), lambda b, s: (jnp.minimum((tile(b, s) + 1) * per, last), 0)),
                  pl.BlockSpec((C, LANES), lambda b, s: (tile(b, s), 0)),
                  pl.BlockSpec((SUBLANES, 3 * HK), lambda b, s: (0, 0)),
                  row1, row1],
        out_specs=pl.BlockSpec((C, HK), lambda b, s: (tile(b, s), 0)),
        out_shape=jax.ShapeDtypeStruct((M, HK), BF16),
        scratch_shapes=[pltpu.VMEM((DN_HEADS, DN_DK, DN_DV), F32),
                        pltpu.VMEM((LANES, C), F32), pltpu.VMEM((LANES, C), F32)],
        compiler_params=_params("arbitrary", "arbitrary"),
        name="deltanet_rev" if reverse else "deltanet_fwd",
    )(p, p, p, p, p, tail, conv_w, alog_row, dtb_row)


def _moe_kernel(be_ref, nu_ref, xs_ref, wg_ref, wu_ref, wd_ref, o_ref, wgb, wub, wdb):
    b = pl.program_id(0)
    e = be_ref[b]
    prev = be_ref[jnp.maximum(b - 1, 0)]
    used = b < nu_ref[0]

    @pl.when(used & ((b == 0) | (e != prev)))
    def _():
        wgb[...] = wg_ref[...].astype(BF16)
        wub[...] = wu_ref[...].astype(BF16)
        wdb[...] = wd_ref[...].astype(BF16)

    @pl.when(used)
    def _():
        x = xs_ref[...].astype(BF16)
        g = jnp.dot(x, wgb[...], preferred_element_type=F32)
        u = jnp.dot(x, wub[...], preferred_element_type=F32)
        a = (_silu(g) * u).astype(BF16)
        o_ref[...] = jnp.dot(a, wdb[...], preferred_element_type=F32)

    @pl.when(jnp.logical_not(used))
    def _():
        o_ref[...] = jnp.zeros_like(o_ref)


def _moe_experts(xs, block_expert, n_used, w_gate, w_up, w_down, layer):
    n_slots, D = xs.shape
    n_blocks = n_slots // MOE_BLOCK
    DE = w_gate.shape[-1]
    grid_spec = pltpu.PrefetchScalarGridSpec(
        num_scalar_prefetch=2,
        grid=(n_blocks,),
        in_specs=[pl.BlockSpec((MOE_BLOCK, D), lambda b, be, nu: (b, 0)),
                  pl.BlockSpec((None, None, D, DE), lambda b, be, nu: (layer, be[b], 0, 0)),
                  pl.BlockSpec((None, None, D, DE), lambda b, be, nu: (layer, be[b], 0, 0)),
                  pl.BlockSpec((None, None, DE, D), lambda b, be, nu: (layer, be[b], 0, 0))],
        out_specs=pl.BlockSpec((MOE_BLOCK, D), lambda b, be, nu: (b, 0)),
        scratch_shapes=[pltpu.VMEM((D, DE), BF16), pltpu.VMEM((D, DE), BF16), pltpu.VMEM((DE, D), BF16)],
    )
    return pl.pallas_call(
        _moe_kernel,
        grid_spec=grid_spec,
        out_shape=jax.ShapeDtypeStruct((n_slots, D), F32),
        compiler_params=_params("arbitrary"),
        name="moe_experts",
    )(block_expert.astype(jnp.int32), n_used.astype(jnp.int32).reshape(1), xs, w_gate, w_up, w_down)


RANK_BLOCK = 128


def _moe(h2, route, w_gate, w_up, w_down, layer):
    N = h2.shape[0]
    NK = N * TOP_K
    assert NK % RANK_BLOCK == 0
    flat_e = route[:, :TOP_K].astype(jnp.int32).reshape(NK)
    onehot = flat_e[:, None] == jnp.arange(N_EXPERTS, dtype=jnp.int32)[None, :]
    oh = onehot.astype(BF16).reshape(NK // RANK_BLOCK, RANK_BLOCK, N_EXPERTS)
    tri = jnp.tril(jnp.ones((RANK_BLOCK, RANK_BLOCK), BF16))
    within = jnp.einsum('ij,bjk->bik', tri, oh, preferred_element_type=F32)
    tot = within[:, -1, :]
    before = jnp.cumsum(tot, axis=0) - tot
    ranks = (within + before[:, None, :]).reshape(NK, N_EXPERTS)
    rank = jnp.sum(jnp.where(onehot, ranks, 0.0), axis=1).astype(jnp.int32) - 1
    counts = (before[-1] + tot[-1]).astype(jnp.int32)
    padded = (counts + MOE_BLOCK - 1) // MOE_BLOCK * MOE_BLOCK
    pad_end = jnp.cumsum(padded)
    pad_start = pad_end - padded
    dest = pad_start[flat_e] + rank
    n_blocks = (NK + N_EXPERTS * (MOE_BLOCK - 1)) // MOE_BLOCK
    n_slots = n_blocks * MOE_BLOCK
    slot_tok = (jnp.arange(n_slots, dtype=jnp.int32) % N).at[dest].set(jnp.arange(NK, dtype=jnp.int32) // TOP_K)
    block_start = jnp.arange(n_blocks, dtype=jnp.int32) * MOE_BLOCK
    block_expert = jnp.minimum(jnp.sum((pad_end[None, :] <= block_start[:, None]).astype(jnp.int32), axis=1),
                               N_EXPERTS - 1)
    xs = h2[slot_tok]
    ys = _moe_experts(xs, block_expert, pad_end[-1] // MOE_BLOCK, w_gate, w_up, w_down, layer)
    dest2 = dest.reshape(N, TOP_K)
    return [ys[dest2[:, k]] for k in range(TOP_K)]


def _tile_mod(mod_rows, n_batch, tiles_per_batch, n_ctx_tiles):
    D = mod_rows.shape[1] // 6
    t = jnp.arange(n_batch * tiles_per_batch)
    src = jnp.where(t % tiles_per_batch < n_ctx_tiles, n_batch, t // tiles_per_batch)
    table = mod_rows.reshape(mod_rows.shape[0], 6, D)[src]
    return jnp.pad(table, ((0, 0), (0, 2), (0, 0)))


def kernel(x, c, ctx, c_ctx, mod_w, mod_b, norm_mix, norm_ffn, lru_w_in, lru_conv_w, lru_conv_b, lru_gate_a_w, lru_gate_a_b, lru_gate_x_w, lru_gate_x_b, lru_lambda, lru_w_out, dn_w_in, dn_conv_w, dn_a_log, dn_dt_bias, dn_norm, dn_w_out, ret_w_in, ret_norm, ret_w_out, att_w_in, att_q_norm, att_k_norm, att_sink, att_w_out, router_w, router_b, moe_w_gate, moe_w_up, moe_w_down):
    Bb, S, D = x.shape
    n_ctx = ctx.shape[1]
    TB = n_ctx + S
    M = Bb * TB
    tiles_per_batch = TB // ROW_TILE
    assert n_ctx % ROW_TILE == 0 and S % ROW_TILE == 0 and Bb < 16

    cond = jnp.zeros((16, D), F32).at[:Bb].set(jax.nn.silu(c)).at[Bb].set(jax.nn.silu(c_ctx)).astype(BF16)
    mod_rows = _adaln_rows(cond, mod_w, mod_b)
    modtiles = [_tile_mod(mod_rows[i], Bb, tiles_per_batch, n_ctx // ROW_TILE) for i in range(DEPTH)]
    router_w_f32 = jnp.pad(router_w.astype(F32), ((0, 0), (0, LANES - N_EXPERTS)))
    router_w_hi = router_w_f32.astype(BF16)
    router_w_pad = jnp.stack([router_w_hi, (router_w_f32 - router_w_hi.astype(F32)).astype(BF16)])
    router_b_pad = jnp.pad(router_b.astype(F32), (0, LANES - N_EXPERTS))[None]
    rope_att = _rope_tables(n_ctx, S, ATT_HD)
    rope_ret = _rope_tables(n_ctx, S, RET_DK)

    def row(width, col=0, dtype_rows=ROW_TILE):
        return pl.BlockSpec((dtype_rows, width), lambda t: (t, col))

    def vec(width):
        return pl.BlockSpec((1, width), lambda t: (0, 0))

    tm_proj = M // 8 if M % (8 * 16) == 0 else 512
    xa, h = _prenorm(ctx, x, norm_mix[0][None], modtiles[0])
    for i in range(DEPTH):
        kind, j = i % N_MIXERS, i // N_MIXERS
        tm = ROW_TILE
        if kind == 0:
            p = _matmul(h, lru_w_in, layer=j, tm=tm_proj)
            cw = jnp.pad(lru_conv_w[j], ((0, SUBLANES - CONV_W), (0, 0)))
            dirs = [_rglru(p, cw, lru_conv_b[j][None], lru_gate_a_w[j, d].astype(BF16), lru_gate_a_b[j, d][None],
                           lru_gate_x_w[j, d].astype(BF16), lru_gate_x_b[j, d][None], lru_lambda[j, d][None],
                           Bb, n_ctx, S, reverse=(d == 1)) for d in range(2)]
            fin = (_finish_lru, [p] + dirs, [row(LRU_WIDTH, 0), row(LRU_WIDTH), row(LRU_WIDTH)])
            w_out = lru_w_out[j]
        elif kind == 1:
            HK = DN_HEADS * DN_DK
            p = _matmul(h, dn_w_in, layer=j, n_out=4 * HK, tm=tm_proj)
            tail = _matmul(h, jnp.pad(dn_w_in[j][:, 4 * HK:], ((0, 0), (0, LANES - 4 * DN_HEADS))), tm=tm_proj)
            cw = jnp.pad(dn_conv_w[j], ((0, SUBLANES - CONV_W), (0, 0)))
            alog_row = jnp.zeros((1, LANES), F32).at[0, 2 * DN_HEADS:4 * DN_HEADS].set(dn_a_log[j].reshape(-1))
            dtb_row = jnp.zeros((1, LANES), F32).at[0, 2 * DN_HEADS:4 * DN_HEADS].set(dn_dt_bias[j].reshape(-1))
            dirs = [_deltanet(p, tail, cw, alog_row, dtb_row, Bb, n_ctx, S, d) for d in range(2)]
            fin = (_finish_dn, [p] + dirs + [jnp.tile(dn_norm[j], DN_HEADS)[None]],
                   [row(HK, 3), row(HK), row(HK), vec(HK)])
            w_out = dn_w_out[j]
        elif kind == 2:
            tm = ROW_TILE // 2
            HV = RET_HEADS * RET_DV
            p = _matmul(h, ret_w_in, layer=j, tm=tm_proj)
            dirs = [_retention(p, rope_ret[0], rope_ret[1], Bb, n_ctx, S, reverse=(d == 1)) for d in range(2)]
            fin = (_finish_ret, [p] + dirs + [ret_norm[j][None]],
                   [row(HV, 2, tm), row(HV, 0, tm), row(HV, 0, tm), vec(HV)])
            w_out = ret_w_out[j]
        else:
            p = _matmul(h, att_w_in, layer=j, tm=tm_proj)
            q, k, v = _att_prep(p, att_q_norm[j][None], att_k_norm[j][None], rope_att[0], rope_att[1],
                                tiles_per_batch)
            o = _attention(q, k, v, att_sink[j], Bb, n_ctx, S)
            fin = (_finish_att, [o], [row(ATT_HEADS * ATT_HD)])
            w_out = att_w_out[j]

        xa, h2, route = _outproj(fin[0], fin[1], fin[2], w_out.astype(BF16), xa, norm_ffn[i][None],
                                 modtiles[i], router_w_pad, router_b_pad, tm)
        ys = _moe(h2, route, moe_w_gate, moe_w_up, moe_w_down, i)
        if i + 1 < DEPTH:
            xa, h = _resid_norm(xa, ys, route, norm_mix[i + 1][None], modtiles[i], modtiles[i + 1])
        else:
            out = _resid_latent(xa, ys, route, modtiles[i], Bb, n_ctx, S)
    return out.reshape(Bb, S, D)
```

```python
import functools
import math

import jax
import jax.numpy as jnp
from jax import lax
from jax.experimental import pallas as pl
from jax.experimental.pallas import tpu as pltpu

D_MODEL = 2048
DEPTH = 4
GRID_W = 64
N_MIXERS = 4
NORM_EPS = 1e-6
ROPE_BASE = 10000.0
CONV_W = 4

LRU_WIDTH = D_MODEL
LRU_BLOCKS = 8
LRU_BLOCK = LRU_WIDTH // LRU_BLOCKS
LRU_C = 8.0

DN_HEADS = 16
DN_DK = D_MODEL // DN_HEADS
DN_DV = D_MODEL // DN_HEADS

RET_HEADS = 8
RET_DK = D_MODEL // RET_HEADS
RET_DV = 2 * D_MODEL // RET_HEADS

ATT_HEADS = 16
ATT_KV_HEADS = 4
ATT_HD = D_MODEL // ATT_HEADS
ATT_G = ATT_HEADS // ATT_KV_HEADS
WINDOW = 128

N_EXPERTS = 32
N_GROUPS = 8
EXPERTS_PER_GROUP = N_EXPERTS // N_GROUPS
TOP_K = 2
D_EXPERT = 512
MOE_BLOCK = 256

F32 = jnp.float32
BF16 = jnp.bfloat16

VMEM_LIMIT_BYTES = 56 * 1024 * 1024
SUBLANES = 8
LANES = 128
ROW_TILE = 256
HALO = SUBLANES


def _params(*sem):
    return pltpu.CompilerParams(dimension_semantics=sem, vmem_limit_bytes=VMEM_LIMIT_BYTES)


def _sigmoid(x):
    return 0.5 * jnp.tanh(0.5 * x) + 0.5


def _silu(x):
    return x * _sigmoid(x)


def _softplus(x):
    return jnp.maximum(x, 0.0) + jnp.log(1.0 + jnp.exp(-jnp.abs(x)))


def _mm_kernel(x_ref, w_ref, o_ref, wb_ref):
    @pl.when(pl.program_id(1) == 0)
    def _():
        wb_ref[...] = w_ref[...].astype(BF16)

    o_ref[...] = jnp.dot(x_ref[...], wb_ref[...], preferred_element_type=F32)


def _matmul(x, w, layer=None, n_out=None, tm=512, tn=1024):
    M, K = x.shape
    N = w.shape[-1] if n_out is None else n_out
    tm = min(tm, M)
    tn = min(tn, N)
    assert M % tm == 0 and N % tn == 0, (M, N, tm, tn)
    if layer is None:
        w_spec = pl.BlockSpec((K, tn), lambda n, m: (0, n))
    else:
        w_spec = pl.BlockSpec((None, K, tn), lambda n, m: (layer, 0, n))
    return pl.pallas_call(
        _mm_kernel,
        grid=(N // tn, M // tm),
        in_specs=[pl.BlockSpec((tm, K), lambda n, m: (m, 0)), w_spec],
        out_specs=pl.BlockSpec((tm, tn), lambda n, m: (m, n)),
        out_shape=jax.ShapeDtypeStruct((M, N), F32),
        scratch_shapes=[pltpu.VMEM((K, tn), BF16)],
        compiler_params=_params("arbitrary", "arbitrary"),
        name="dense_matmul",
    )(x, w)


def _mod_kernel(c_ref, w_ref, b_ref, o_ref):
    o_ref[...] = jnp.dot(c_ref[...], w_ref[...].astype(BF16), preferred_element_type=F32) + b_ref[...]


def _adaln_rows(cond, mod_w, mod_b, tn=2048):
    R, D = cond.shape
    L, _, N = mod_w.shape
    return pl.pallas_call(
        _mod_kernel,
        grid=(L, N // tn),
        in_specs=[pl.BlockSpec((R, D), lambda l, n: (0, 0)),
                  pl.BlockSpec((None, D, tn), lambda l, n: (l, 0, n)),
                  pl.BlockSpec((None, 1, tn), lambda l, n: (l, 0, n))],
        out_specs=pl.BlockSpec((None, R, tn), lambda l, n: (l, 0, n)),
        out_shape=jax.ShapeDtypeStruct((L, R, N), F32),
        compiler_params=_params("arbitrary", "arbitrary"),
        name="adaln_rows",
    )(cond, mod_w, mod_b.reshape(L, 1, N))


def _norm_mod(x, normw, shift, scale):
    y = x * lax.rsqrt(jnp.mean(x * x, axis=-1, keepdims=True) + NORM_EPS)
    return (y * normw) * (1.0 + scale) + shift


def _prenorm_kernel(tiles_per_batch, n_ctx_tiles, ctx_ref, x_ref, nw_ref, mod_ref, xa_ref, h_ref):
    is_ctx = (pl.program_id(0) % tiles_per_batch) < n_ctx_tiles
    xa = jnp.where(is_ctx, ctx_ref[...], x_ref[...])
    xa_ref[...] = xa
    h_ref[...] = _norm_mod(xa, nw_ref[...], mod_ref[0:1, :], mod_ref[1:2, :]).astype(BF16)


def _prenorm(ctx, x, normw, modtile):
    Bb, n_ctx, D = ctx.shape
    S = x.shape[1]
    nct, nxt = n_ctx // ROW_TILE, S // ROW_TILE
    tpb = nct + nxt
    M = Bb * (n_ctx + S)
    row = pl.BlockSpec((ROW_TILE, D), lambda t: (t, 0))
    return pl.pallas_call(
        functools.partial(_prenorm_kernel, tpb, nct),
        grid=(Bb * tpb,),
        in_specs=[pl.BlockSpec((None, ROW_TILE, D), lambda t: (t // tpb, jnp.minimum(t % tpb, nct - 1), 0)),
                  pl.BlockSpec((None, ROW_TILE, D), lambda t: (t // tpb, jnp.maximum(t % tpb - nct, 0), 0)),
                  pl.BlockSpec((1, D), lambda t: (0, 0)),
                  pl.BlockSpec((None, 8, D), lambda t: (t, 0, 0))],
        out_specs=[row, row],
        out_shape=[jax.ShapeDtypeStruct((M, D), F32), jax.ShapeDtypeStruct((M, D), BF16)],
        compiler_params=_params("arbitrary"),
        name="prenorm",
    )(ctx, x, normw, modtile)


def _ffn_residual(x_ref, y_refs, rt_ref, mod_ref):
    f = None
    for k, y_ref in enumerate(y_refs):
        lane = TOP_K + k
        term = y_ref[...] * rt_ref[:, lane:lane + 1]
        f = term if f is None else f + term
    return x_ref[...] + mod_ref[5:6, :] * f


def _resid_norm_kernel(x_ref, y0_ref, y1_ref, rt_ref, nw_ref, mod_ref, modn_ref, xo_ref, h_ref):
    x = _ffn_residual(x_ref, (y0_ref, y1_ref), rt_ref, mod_ref)
    xo_ref[...] = x
    h_ref[...] = _norm_mod(x, nw_ref[...], modn_ref[0:1, :], modn_ref[1:2, :]).astype(BF16)


def _resid_norm(x, ys, route, normw_next, modtile, modtile_next):
    M, D = x.shape
    row = pl.BlockSpec((ROW_TILE, D), lambda t: (t, 0))
    mod = pl.BlockSpec((None, 8, D), lambda t: (t, 0, 0))
    return pl.pallas_call(
        _resid_norm_kernel,
        grid=(M // ROW_TILE,),
        in_specs=[row, row, row, pl.BlockSpec((ROW_TILE, LANES), lambda t: (t, 0)),
                  pl.BlockSpec((1, D), lambda t: (0, 0)), mod, mod],
        out_specs=[row, row],
        out_shape=[jax.ShapeDtypeStruct((M, D), F32), jax.ShapeDtypeStruct((M, D), BF16)],
        compiler_params=_params("arbitrary"),
        name="resid_norm",
    )(x, ys[0], ys[1], route, normw_next, modtile, modtile_next)


def _resid_kernel(x_ref, y0_ref, y1_ref, rt_ref, mod_ref, xo_ref):
    xo_ref[...] = _ffn_residual(x_ref, (y0_ref, y1_ref), rt_ref, mod_ref)


def _resid_latent(x, ys, route, modtile, n_batch, n_ctx, seq):
    M, D = x.shape
    nct, nxt = n_ctx // ROW_TILE, seq // ROW_TILE
    tpb = nct + nxt
    row = pl.BlockSpec((ROW_TILE, D), lambda t: (t, 0))
    out = pl.BlockSpec((ROW_TILE, D), lambda t: ((t // tpb) * nxt + jnp.maximum(t % tpb - nct, 0), 0))
    return pl.pallas_call(
        _resid_kernel,
        grid=(M // ROW_TILE,),
        in_specs=[row, row, row, pl.BlockSpec((ROW_TILE, LANES), lambda t: (t, 0)),
                  pl.BlockSpec((None, 8, D), lambda t: (t, 0, 0))],
        out_specs=out,
        out_shape=jax.ShapeDtypeStruct((n_batch * seq, D), F32),
        compiler_params=_params("arbitrary"),
        name="resid",
    )(x, ys[0], ys[1], route, modtile)


def _head_rms(o, width, normw):
    parts = []
    for h in range(o.shape[1] // width):
        oh = o[:, h * width:(h + 1) * width]
        parts.append(oh * lax.rsqrt(jnp.mean(oh * oh, axis=-1, keepdims=True) + NORM_EPS))
    return jnp.concatenate(parts, axis=1) * normw


def _both_directions(f_ref, b_ref):
    return f_ref[...].astype(F32) + b_ref[...].astype(F32)


def _finish_lru(gate_ref, hf_ref, hb_ref):
    return jax.nn.gelu(gate_ref[...]) * _both_directions(hf_ref, hb_ref)


def _finish_dn(z_ref, of_ref, ob_ref, nw_ref):
    return _head_rms(_both_directions(of_ref, ob_ref), DN_DV, nw_ref[...]) * _silu(z_ref[...])


def _finish_ret(gate_ref, of_ref, ob_ref, nw_ref):
    return _head_rms(_both_directions(of_ref, ob_ref), RET_DV, nw_ref[...]) * _silu(gate_ref[...])


def _finish_att(o_ref):
    return o_ref[...]


ROUTE_WEIGHT_LANE = TOP_K


def _route_tile(logits, bias):
    assert TOP_K == 2 and EXPERTS_PER_GROUP == 4
    lane = lax.broadcasted_iota(jnp.int32, logits.shape, 1)
    valid = lane < N_EXPERTS
    neg = -jnp.inf
    scores = _sigmoid(logits)
    b = jnp.where(valid, scores + bias, neg)
    j = lane % EXPERTS_PER_GROUP
    jf = j.astype(F32)
    gf = (lane // EXPERTS_PER_GROUP).astype(F32)

    def group_reduce(v, op):
        v = op(v, jnp.where(j % 2 == 0, pltpu.roll(v, LANES - 1, axis=1), pltpu.roll(v, 1, axis=1)))
        return op(v, jnp.where(j < 2, pltpu.roll(v, LANES - 2, axis=1), pltpu.roll(v, 2, axis=1)))

    m1 = group_reduce(b, jnp.maximum)
    is1 = jf == group_reduce(jnp.where(b == m1, jf, float(EXPERTS_PER_GROUP)), jnp.minimum)
    b2 = jnp.where(is1, neg, b)
    m2 = group_reduce(b2, jnp.maximum)
    is2 = jf == group_reduce(jnp.where(b2 == m2, jf, float(EXPERTS_PER_GROUP)), jnp.minimum)
    gscore = jnp.where(valid, m1 + m2, neg)
    best = jnp.max(gscore, axis=-1, keepdims=True)
    gsel = jnp.min(jnp.where(gscore == best, gf, float(N_GROUPS)), axis=-1, keepdims=True)
    in_group = valid & (gf == gsel)
    lf = lane.astype(F32)

    def pick(mask, v):
        return jnp.sum(jnp.where(in_group & mask, v, 0.0), axis=-1, keepdims=True)

    e1, e2 = pick(is1, lf), pick(is2, lf)
    s1, s2 = pick(is1, scores), pick(is2, scores)
    tot = s1 + s2
    return jnp.where(lane == 0, e1, jnp.where(lane == 1, e2, jnp.where(
        lane == ROUTE_WEIGHT_LANE, s1 / tot, jnp.where(lane == ROUTE_WEIGHT_LANE + 1, s2 / tot, 0.0))))


def _outproj_kernel(finish, n_fin, *refs):
    fin_refs = refs[:n_fin]
    w_ref, x_ref, nw_ref, mod_ref, rw_ref, rb_ref, xo_ref, h_ref, rt_ref = refs[n_fin:]
    a = finish(*fin_refs).astype(BF16)
    y = jnp.dot(a, w_ref[...], preferred_element_type=F32)
    x = x_ref[...] + mod_ref[2:3, :] * y
    xo_ref[...] = x
    h = _norm_mod(x, nw_ref[...], mod_ref[3:4, :], mod_ref[4:5, :])
    h_ref[...] = h
    h_hi = h.astype(BF16)
    h_lo = (h - h_hi.astype(F32)).astype(BF16)
    logits = (jnp.dot(h_hi, rw_ref[0], preferred_element_type=F32)
              + jnp.dot(h_lo, rw_ref[0], preferred_element_type=F32)
              + jnp.dot(h_hi, rw_ref[1], preferred_element_type=F32))
    rt_ref[...] = _route_tile(logits, rb_ref[...])


def _outproj(finish, fin_args, fin_specs, w_bf16, x, normw_ffn, modtile, router_w_pad, router_b_pad, tm):
    M, D = x.shape
    K = w_bf16.shape[0]
    per = ROW_TILE // tm
    row = pl.BlockSpec((tm, D), lambda t: (t, 0))
    in_specs = list(fin_specs) + [
        pl.BlockSpec((K, D), lambda t: (0, 0)),
        row,
        pl.BlockSpec((1, D), lambda t: (0, 0)),
        pl.BlockSpec((None, 8, D), lambda t: (t // per, 0, 0)),
        pl.BlockSpec((2, D, LANES), lambda t: (0, 0, 0)),
        pl.BlockSpec((1, LANES), lambda t: (0, 0)),
    ]
    return pl.pallas_call(
        functools.partial(_outproj_kernel, finish, len(fin_args)),
        grid=(M // tm,),
        in_specs=in_specs,
        out_specs=[row, row, pl.BlockSpec((tm, LANES), lambda t: (t, 0))],
        out_shape=[jax.ShapeDtypeStruct((M, D), F32), jax.ShapeDtypeStruct((M, D), F32),
                   jax.ShapeDtypeStruct((M, LANES), F32)],
        compiler_params=_params("arbitrary"),
        name="outproj",
    )(*fin_args, w_bf16, x, normw_ffn, modtile, router_w_pad, router_b_pad)


def _rope_tables(n_ctx, seq, head_dim):
    quarter = head_dim // 4
    pos = jnp.arange(seq)
    inv_freq = ROPE_BASE ** (-jnp.arange(quarter, dtype=F32) / quarter)
    ang_r = (pos // GRID_W).astype(F32)[:, None] * inv_freq
    ang_c = (pos % GRID_W).astype(F32)[:, None] * inv_freq
    c = jnp.concatenate([jnp.cos(ang_r), jnp.cos(ang_r), jnp.cos(ang_c), jnp.cos(ang_c)], axis=1)
    s = jnp.concatenate([-jnp.sin(ang_r), jnp.sin(ang_r), -jnp.sin(ang_c), jnp.sin(ang_c)], axis=1)
    c = jnp.concatenate([jnp.ones((n_ctx, head_dim), F32), c], axis=0)
    s = jnp.concatenate([jnp.zeros((n_ctx, head_dim), F32), s], axis=0)
    return c, s


def _rope(x, c, s, quarter):
    width = x.shape[1]
    if 2 * quarter == LANES:
        parts = [pltpu.roll(x[:, j:j + LANES], quarter, axis=1) for j in range(0, width, LANES)]
        partner = parts[0] if len(parts) == 1 else jnp.concatenate(parts, axis=1)
    else:
        assert width == LANES and 4 * quarter == LANES
        lane = lax.broadcasted_iota(jnp.int32, x.shape, 1)
        partner = jnp.where((lane % (2 * quarter)) < quarter,
                            pltpu.roll(x, LANES - quarter, axis=1), pltpu.roll(x, quarter, axis=1))
    return x * c + partner * s


ATT_TILE = 128


def _att_prep_kernel(p_ref, qn_ref, kn_ref, c_ref, s_ref, q_ref, k_ref, v_ref):
    c = c_ref[...]
    s = s_ref[...]
    qw = ATT_HEADS * ATT_HD
    kw = ATT_KV_HEADS * ATT_HD

    def norm_rope(xh, w):
        y = xh * lax.rsqrt(jnp.mean(xh * xh, axis=-1, keepdims=True) + NORM_EPS) * w
        return _rope(y, c, s, ATT_HD // 4)

    for h in range(ATT_HEADS):
        qh = norm_rope(p_ref[:, h * ATT_HD:(h + 1) * ATT_HD], qn_ref[...])
        q_ref[:, h * ATT_HD:(h + 1) * ATT_HD] = (qh * ATT_HD ** -0.5).astype(BF16)
    for h in range(ATT_KV_HEADS):
        kh = norm_rope(p_ref[:, qw + h * ATT_HD:qw + (h + 1) * ATT_HD], kn_ref[...])
        k_ref[:, h * ATT_HD:(h + 1) * ATT_HD] = kh.astype(BF16)
    v_ref[...] = p_ref[:, qw + kw:qw + 2 * kw].astype(BF16)


def _att_prep(p, q_norm, k_norm, rope_c, rope_s, tiles_per_batch):
    M = p.shape[0]
    qw, kw = ATT_HEADS * ATT_HD, ATT_KV_HEADS * ATT_HD
    tab = pl.BlockSpec((ROW_TILE, ATT_HD), lambda t: (t % tiles_per_batch, 0))
    return pl.pallas_call(
        _att_prep_kernel,
        grid=(M // ROW_TILE,),
        in_specs=[pl.BlockSpec((ROW_TILE, qw + 2 * kw), lambda t: (t, 0)),
                  pl.BlockSpec((1, ATT_HD), lambda t: (0, 0)),
                  pl.BlockSpec((1, ATT_HD), lambda t: (0, 0)), tab, tab],
        out_specs=[pl.BlockSpec((ROW_TILE, qw), lambda t: (t, 0)),
                   pl.BlockSpec((ROW_TILE, kw), lambda t: (t, 0)),
                   pl.BlockSpec((ROW_TILE, kw), lambda t: (t, 0))],
        out_shape=[jax.ShapeDtypeStruct((M, qw), BF16), jax.ShapeDtypeStruct((M, kw), BF16),
                   jax.ShapeDtypeStruct((M, kw), BF16)],
        compiler_params=_params("arbitrary"),
        name="att_prep",
    )(p, q_norm, k_norm, rope_c, rope_s)


def _att_kernel(n_ctx_tiles, n_tiles, q_ref, kp_ref, ko_ref, kn_ref, vp_ref, vo_ref, vn_ref,
                kc_ref, vc_ref, sink_ref, o_ref):
    t = pl.program_id(1)
    T = ATT_TILE
    rows = ATT_G * T
    row = lax.broadcasted_iota(jnp.int32, (rows, 3 * T), 0) % T
    col = lax.broadcasted_iota(jnp.int32, (rows, 3 * T), 1)
    blk = col // T
    c = col % T
    latent = t >= n_ctx_tiles
    prev_ok = latent & (t >= n_ctx_tiles + 1)
    next_ok = latent & (t <= n_tiles - 2)
    band = ((blk == 1) & latent) | ((blk == 0) & (c >= row) & prev_ok) | ((blk == 2) & (c <= row) & next_ok)
    nt = (((1,), (1,)), ((), ()))
    KV = range(ATT_KV_HEADS)
    ks = [slice(kvh * ATT_HD, (kvh + 1) * ATT_HD) for kvh in KV]
    q4 = [jnp.concatenate([q_ref[:, (kvh * ATT_G + g) * ATT_HD:(kvh * ATT_G + g + 1) * ATT_HD]
                           for g in range(ATT_G)], axis=0) for kvh in KV]
    s_loc = [lax.dot_general(q4[i], jnp.concatenate([kp_ref[:, ks[i]], ko_ref[:, ks[i]], kn_ref[:, ks[i]]], axis=0),
                             nt, preferred_element_type=F32) for i in KV]
    s_ctx = [lax.dot_general(q4[i], kc_ref[:, ks[i]], nt, preferred_element_type=F32) for i in KV]
    s_loc = [jnp.where(band, s_loc[i], -jnp.inf) for i in KV]
    sink = [sink_ref[i][:, 0:1] for i in KV]
    m = [jnp.maximum(sink[i], jnp.maximum(jnp.max(s_loc[i], axis=-1, keepdims=True),
                                          jnp.max(s_ctx[i], axis=-1, keepdims=True))) for i in KV]
    p_loc = [jnp.exp(s_loc[i] - m[i]) for i in KV]
    p_ctx = [jnp.exp(s_ctx[i] - m[i]) for i in KV]
    den = [jnp.exp(sink[i] - m[i]) + jnp.sum(p_loc[i], axis=-1, keepdims=True)
           + jnp.sum(p_ctx[i], axis=-1, keepdims=True) for i in KV]
    o = [jnp.dot(p_loc[i].astype(BF16),
                 jnp.concatenate([vp_ref[:, ks[i]], vo_ref[:, ks[i]], vn_ref[:, ks[i]]], axis=0),
                 preferred_element_type=F32)
         + jnp.dot(p_ctx[i].astype(BF16), vc_ref[:, ks[i]], preferred_element_type=F32) for i in KV]
    for i in KV:
        oi = o[i] / den[i]
        for g in range(ATT_G):
            h = i * ATT_G + g
            o_ref[:, h * ATT_HD:(h + 1) * ATT_HD] = oi[g * T:(g + 1) * T, :].astype(BF16)


def _attention(q, k, v, sink, n_batch, n_ctx, seq):
    M, qw = q.shape
    kw = k.shape[1]
    T = ATT_TILE
    n_tiles = (n_ctx + seq) // T
    n_ctx_tiles = n_ctx // T
    sink_rows = jnp.broadcast_to(
        jnp.repeat(sink.astype(F32).reshape(ATT_KV_HEADS, ATT_G), T, axis=1)[:, :, None],
        (ATT_KV_HEADS, ATT_G * T, LANES))

    def tile(off):
        return pl.BlockSpec((T, kw), lambda b, t: (b * n_tiles + jnp.clip(t + off, 0, n_tiles - 1), 0))

    ctx = pl.BlockSpec((n_ctx, kw), lambda b, t: (b * (n_tiles // n_ctx_tiles), 0))
    return pl.pallas_call(
        functools.partial(_att_kernel, n_ctx_tiles, n_tiles),
        grid=(n_batch, n_tiles),
        in_specs=[pl.BlockSpec((T, qw), lambda b, t: (b * n_tiles + t, 0)),
                  tile(-1), tile(0), tile(1), tile(-1), tile(0), tile(1), ctx, ctx,
                  pl.BlockSpec((ATT_KV_HEADS, ATT_G * T, LANES), lambda b, t: (0, 0, 0))],
        out_specs=pl.BlockSpec((T, qw), lambda b, t: (b * n_tiles + t, 0)),
        out_shape=jax.ShapeDtypeStruct((M, qw), BF16),
        compiler_params=_params("arbitrary", "arbitrary"),
        name="window_attention",
    )(q, k, k, k, v, v, v, k, v, sink_rows)


def _chunk_of_step(s, n_chunks, n_ctx_chunks, reverse):
    if not reverse:
        return s
    return jnp.where(s < n_ctx_chunks, n_ctx_chunks - 1 - s, n_chunks - 1 - (s - n_ctx_chunks))


RET_CHUNK = ROW_TILE


def _ret_kernel(reverse, q_ref, k_ref, v_ref, c_ref, s_ref, o_ref, state_ref):
    @pl.when(pl.program_id(1) == 0)
    def _():
        state_ref[...] = jnp.zeros_like(state_ref)

    C = RET_CHUNK
    c = c_ref[...]
    s = s_ref[...]
    i = lax.broadcasted_iota(jnp.int32, (C, C), 0)
    j = lax.broadcasted_iota(jnp.int32, (C, C), 1)
    pos = lax.broadcasted_iota(jnp.int32, (C, 1), 0).astype(F32)
    diff = (j - i) if reverse else (i - j)
    mask = diff > 0 if reverse else diff >= 0
    dist = jnp.maximum(diff, 0).astype(F32)
    steps_in = (C - pos) if reverse else (pos + 1.0)
    steps_out = pos if reverse else (C - 1.0 - pos)
    nt = (((1,), (1,)), ((), ()))
    for h in range(RET_HEADS):
        log_gamma = math.log1p(-2.0 ** (-5.0 - h))
        q = _rope(q_ref[:, h * RET_DK:(h + 1) * RET_DK], c, s, RET_DK // 4)
        k = _rope(k_ref[:, h * RET_DK:(h + 1) * RET_DK], c, s, RET_DK // 4) * RET_DK ** -0.5
        v = v_ref[:, h * RET_DV:(h + 1) * RET_DV].astype(BF16)
        qb = q.astype(BF16)
        dmat = jnp.where(mask, jnp.exp(dist * log_gamma), 0.0)
        a = lax.dot_general(qb, k.astype(BF16), nt, preferred_element_type=F32) * dmat
        st = state_ref[h]
        o = jnp.dot(a.astype(BF16), v, preferred_element_type=F32)
        o = o + jnp.dot(qb, st.astype(BF16), preferred_element_type=F32) * jnp.exp(steps_in * log_gamma)
        o_ref[:, h * RET_DV:(h + 1) * RET_DV] = o.astype(BF16)
        kt = (k * jnp.exp(steps_out * log_gamma)).T.astype(BF16)
        state_ref[h] = st * math.exp(C * log_gamma) + jnp.dot(kt, v, preferred_element_type=F32)


def _retention(p, rope_c, rope_s, n_batch, n_ctx, seq, reverse):
    M = p.shape[0]
    C = RET_CHUNK
    n_chunks = (n_ctx + seq) // C
    n_ctx_chunks = n_ctx // C
    hk, hv = RET_HEADS * RET_DK, RET_HEADS * RET_DV

    def chunk(b, s):
        return _chunk_of_step(s, n_chunks, n_ctx_chunks, reverse)

    tab = pl.BlockSpec((C, RET_DK), lambda b, s: (chunk(b, s), 0))
    return pl.pallas_call(
        functools.partial(_ret_kernel, reverse),
        grid=(n_batch, n_chunks),
        in_specs=[pl.BlockSpec((C, hk), lambda b, s: (b * n_chunks + chunk(b, s), 0)),
                  pl.BlockSpec((C, hk), lambda b, s: (b * n_chunks + chunk(b, s), 1)),
                  pl.BlockSpec((C, hv), lambda b, s: (b * n_chunks + chunk(b, s), 1)),
                  tab, tab],
        out_specs=pl.BlockSpec((C, hv), lambda b, s: (b * n_chunks + chunk(b, s), 0)),
        out_shape=jax.ShapeDtypeStruct((M, hv), BF16),
        scratch_shapes=[pltpu.VMEM((RET_HEADS, RET_DK, RET_DV), F32)],
        compiler_params=_params("arbitrary", "arbitrary"),
        name="retention_rev" if reverse else "retention_fwd",
    )(p, p, p, rope_c, rope_s)


LRU_CHUNK = ROW_TILE


def _lru_kernel(reverse, n_chunks, n_ctx_chunks, x_ref, xp_ref, xn_ref, cw_ref, cb_ref, wa_ref, ba_ref,
                wx_ref, bx_ref, lam_ref, o_ref, h_ref):
    step = pl.program_id(1)

    @pl.when(step == 0)
    def _():
        h_ref[...] = jnp.zeros_like(h_ref)

    C = LRU_CHUNK
    W = LRU_BLOCK
    chunk = _chunk_of_step(step, n_chunks, n_ctx_chunks, reverse)
    has_prev = ((chunk != 0) & (chunk != n_ctx_chunks)).astype(F32)
    has_next = ((chunk != n_ctx_chunks - 1) & (chunk != n_chunks - 1)).astype(F32)
    rows = C + 2 * HALO
    sub = lax.broadcasted_iota(jnp.int32, (C, W), 0) % SUBLANES

    def block(n, carry):
        lanes = pl.ds(pl.multiple_of(n * W, W), W)
        full = jnp.concatenate([xp_ref[:, lanes] * has_prev, x_ref[:, lanes], xn_ref[:, lanes] * has_next], axis=0)
        cw = cw_ref[:, lanes]
        xc = (cw[0:1] * pltpu.roll(full, 2, axis=0)[HALO:HALO + C]
              + cw[1:2] * pltpu.roll(full, 1, axis=0)[HALO:HALO + C]
              + cw[2:3] * full[HALO:HALO + C]
              + cw[3:4] * pltpu.roll(full, rows - 1, axis=0)[HALO:HALO + C]) + cb_ref[:, lanes]
        xb = xc.astype(BF16)
        r = _sigmoid(jnp.dot(xb, wa_ref[n], preferred_element_type=F32) + ba_ref[:, lanes])
        gi = _sigmoid(jnp.dot(xb, wx_ref[n], preferred_element_type=F32) + bx_ref[:, lanes])
        a = jnp.exp((-LRU_C) * r * _softplus(-lam_ref[:, lanes]))
        b = jnp.sqrt(jnp.maximum(1.0 - a * a, 0.0)) * (gi * xc)
        for k in (1, 2, 4):
            if reverse:
                keep = sub < SUBLANES - k
                a_sh = pltpu.roll(a, C - k, axis=0)
                b_sh = pltpu.roll(b, C - k, axis=0)
            else:
                keep = sub >= k
                a_sh = pltpu.roll(a, k, axis=0)
                b_sh = pltpu.roll(b, k, axis=0)
            b = b + a * jnp.where(keep, b_sh, 0.0)
            a = a * jnp.where(keep, a_sh, 1.0)
        hin = h_ref[0:1, lanes]
        groups = C // SUBLANES
        outs = [None] * groups
        for g in (range(groups - 1, -1, -1) if reverse else range(groups)):
            lo = g * SUBLANES
            hg = b[lo:lo + SUBLANES] + a[lo:lo + SUBLANES] * hin
            outs[g] = hg
            hin = hg[0:1] if reverse else hg[SUBLANES - 1:SUBLANES]
        o_ref[:, lanes] = jnp.concatenate(outs, axis=0).astype(BF16)
        h_ref[:, lanes] = jnp.broadcast_to(hin, (SUBLANES, W))
        return carry

    lax.fori_loop(0, LRU_BLOCKS, block, 0)


def _rglru(p, conv_w, conv_b, gate_a_w, gate_a_b, gate_x_w, gate_x_b, lam, n_batch, n_ctx, seq, reverse):
    M = p.shape[0]
    C = LRU_CHUNK
    Wd = LRU_WIDTH
    n_chunks = (n_ctx + seq) // C
    n_ctx_chunks = n_ctx // C
    per = C // HALO
    last = M // HALO - 1

    def tile(b, s):
        return b * n_chunks + _chunk_of_step(s, n_chunks, n_ctx_chunks, reverse)

    vec = pl.BlockSpec((1, Wd), lambda b, s: (0, 0))
    wts = pl.BlockSpec((LRU_BLOCKS, LRU_BLOCK, LRU_BLOCK), lambda b, s: (0, 0, 0))
    return pl.pallas_call(
        functools.partial(_lru_kernel, reverse, n_chunks, n_ctx_chunks),
        grid=(n_batch, n_chunks),
        in_specs=[pl.BlockSpec((C, Wd), lambda b, s: (tile(b, s), 1)),
                  pl.BlockSpec((HALO, Wd), lambda b, s: (jnp.maximum(tile(b, s) * per - 1, 0), 1)),
                  pl.BlockSpec((HALO, Wd), lambda b, s: (jnp.minimum((tile(b, s) + 1) * per, last), 1)),
                  pl.BlockSpec((SUBLANES, Wd), lambda b, s: (0, 0)),
                  vec, wts, vec, wts, vec, vec],
        out_specs=pl.BlockSpec((C, Wd), lambda b, s: (tile(b, s), 0)),
        out_shape=jax.ShapeDtypeStruct((M, Wd), BF16),
        scratch_shapes=[pltpu.VMEM((SUBLANES, Wd), F32)],
        compiler_params=_params("arbitrary", "arbitrary"),
        name="rglru_rev" if reverse else "rglru_fwd",
    )(p, p, p, conv_w, conv_b, gate_a_w, gate_a_b, gate_x_w, gate_x_b, lam)


DN_CHUNK = 128
DN_HEAD_GROUP = 16
DN_INV_BASE = SUBLANES
DN_NEUMANN_ROUNDS = 2


def _dn_kernel(reverse, d, n_chunks, n_ctx_chunks, q_ref, k_ref, v_ref, xp_ref, xn_ref, tail_ref, cw_ref,
               alog_ref, dtb_ref, o_ref, state_ref, gct_ref, bt_ref):
    step = pl.program_id(1)

    @pl.when(step == 0)
    def _():
        state_ref[...] = jnp.zeros_like(state_ref)

    C = DN_CHUNK
    H = DN_HEADS
    HK = H * DN_DK
    chunk = _chunk_of_step(step, n_chunks, n_ctx_chunks, reverse)
    has_prev = ((chunk != 0) & (chunk != n_ctx_chunks)).astype(F32)
    has_next = ((chunk != n_ctx_chunks - 1) & (chunk != n_chunks - 1)).astype(F32)
    rows = C + 2 * HALO
    ci = lax.broadcasted_iota(jnp.int32, (C, C), 0)
    si = lax.broadcasted_iota(jnp.int32, (C, C), 1)
    mask = (si >= ci) if reverse else (si <= ci)
    strict = (si > ci) if reverse else (si < ci)
    eye = (si == ci).astype(F32)
    last = 0 if reverse else C - 1
    same_base = (ci // DN_INV_BASE) == (si // DN_INV_BASE)
    merge_masks = []
    size = DN_INV_BASE
    while size < C:
        cb, sb = ci // size, si // size
        merge_masks.append(((cb % 2 == 0) & (sb == cb + 1)) if reverse else ((cb % 2 == 1) & (sb == cb - 1)))
        size *= 2

    tail = tail_ref[...]
    beta = _sigmoid(tail)
    g = -jnp.exp(alog_ref[...]) * _softplus(tail + dtb_ref[...])
    gcum = jnp.dot(mask.astype(F32), g, preferred_element_type=F32, precision=lax.Precision.HIGHEST)
    gct_ref[...] = gcum.T
    bt_ref[...] = beta.T
    nt = (((1,), (1,)), ((), ()))

    def conv_silu(ref, ref_off, h):
        lanes_in = pl.ds(pl.multiple_of(h * DN_DK, DN_DK), DN_DK)
        lanes_all = pl.ds(pl.multiple_of(ref_off + h * DN_DK, DN_DK), DN_DK)
        full = jnp.concatenate([xp_ref[:, lanes_all] * has_prev, ref[:, lanes_in], xn_ref[:, lanes_all] * has_next],
                               axis=0)
        cw = cw_ref[:, lanes_all]
        y = (cw[0:1] * pltpu.roll(full, 2, axis=0)[HALO:HALO + C]
             + cw[1:2] * pltpu.roll(full, 1, axis=0)[HALO:HALO + C]
             + cw[2:3] * full[HALO:HALO + C]
             + cw[3:4] * pltpu.roll(full, rows - 1, axis=0)[HALO:HALO + C])
        return _silu(y)

    def l2n(x):
        return x * lax.rsqrt(jnp.sum(x * x, axis=-1, keepdims=True) + NORM_EPS)

    def dot(a, b):
        return jnp.dot(a.astype(BF16), b.astype(BF16), preferred_element_type=F32)

    def dot_nt(a, b):
        return lax.dot_general(a.astype(BF16), b.astype(BF16), nt, preferred_element_type=F32)

    def group(gi, carry):
        hs = [gi * DN_HEAD_GROUP + jj for jj in range(DN_HEAD_GROUP)]
        G = range(DN_HEAD_GROUP)
        sts = [state_ref[h] for h in hs]
        g_row = [gct_ref[pl.ds(2 * H + d * H + h, 1), :] for h in hs]
        g_rows = [jnp.broadcast_to(g_row[i], (C, C)) for i in G]
        g_cols = [g_rows[i].T for i in G]
        b_cols = [jnp.broadcast_to(bt_ref[pl.ds(d * H + h, 1), :], (C, C)).T for h in hs]
        g_last = [g_row[i][:, last:last + 1] for i in G]
        decay = [jnp.where(mask, jnp.exp(jnp.where(mask, g_cols[i] - g_rows[i], 0.0)), 0.0) for i in G]
        q = [l2n(conv_silu(q_ref, 0, h)) * DN_DK ** -0.5 for h in hs]
        k = [l2n(conv_silu(k_ref, HK, h)) for h in hs]
        v = [conv_silu(v_ref, 2 * HK, h) for h in hs]
        kb = [k[i] * b_cols[i] for i in G]
        m = [jnp.where(strict, dot_nt(kb[i], k[i]) * decay[i], 0.0) for i in G]
        a_intra = [dot_nt(q[i], k[i]) * decay[i] for i in G]

        n = [jnp.where(same_base, -m[i], 0.0) for i in G]
        t = [eye + n[i] for i in G]
        for _ in range(DN_NEUMANN_ROUNDS):
            n = [dot(n[i], n[i]) for i in G]
            t = [t[i] + dot(t[i], n[i]) for i in G]
        for pair in merge_masks:
            x = [dot(t[i], jnp.where(pair, m[i], 0.0)) for i in G]
            t = [t[i] - dot(x[i], t[i]) for i in G]
        sol = [dot(t[i], jnp.concatenate([v[i] * b_cols[i], kb[i] * jnp.exp(g_cols[i])], axis=1)) for i in G]

        r = [dot(jnp.concatenate([sol[i][:, DN_DV:], q[i] * jnp.exp(g_cols[i])], axis=0), sts[i]) for i in G]
        v_new = [sol[i][:, :DN_DV] - r[i][:C] for i in G]
        o = [r[i][C:] + dot(a_intra[i], v_new[i]) for i in G]
        st_new = [sts[i] * jnp.exp(g_last[i]) + dot((k[i] * jnp.exp(g_last[i] - g_cols[i])).T, v_new[i]) for i in G]
        for i, h in enumerate(hs):
            o_ref[:, pl.ds(pl.multiple_of(h * DN_DV, DN_DV), DN_DV)] = o[i].astype(BF16)
            state_ref[h] = st_new[i]
        return carry

    lax.fori_loop(0, H // DN_HEAD_GROUP, group, 0)


def _deltanet(p, tail, conv_w, alog_row, dtb_row, n_batch, n_ctx, seq, d):
    M = p.shape[0]
    C = DN_CHUNK
    reverse = d == 1
    n_chunks = (n_ctx + seq) // C
    n_ctx_chunks = n_ctx // C
    HK = DN_HEADS * DN_DK
    per = C // HALO
    last = M // HALO - 1

    def tile(b, s):
        return b * n_chunks + _chunk_of_step(s, n_chunks, n_ctx_chunks, reverse)

    row1 = pl.BlockSpec((1, LANES), lambda b, s: (0, 0))
    return pl.pallas_call(
        functools.partial(_dn_kernel, reverse, d, n_chunks, n_ctx_chunks),
        grid=(n_batch, n_chunks),
        in_specs=[pl.BlockSpec((C, HK), lambda b, s: (tile(b, s), 0)),
                  pl.BlockSpec((C, HK), lambda b, s: (tile(b, s), 1)),
                  pl.BlockSpec((C, HK), lambda b, s: (tile(b, s), 2)),
                  pl.BlockSpec((HALO, 4 * HK), lambda b, s: (jnp.maximum(tile(b, s) * per - 1, 0), 0)),
                  pl.BlockSpec((HALO, 4 * HK), lambda b, s: (jnp.minimum((tile(b, s) + 1) * per, last), 0)),
                  pl.BlockSpec((C, LANES), lambda b, s: (tile(b, s), 0)),
                  pl.BlockSpec((SUBLANES, 3 * HK), lambda b, s: (0, 0)),
                  row1, row1],
        out_specs=pl.BlockSpec((C, HK), lambda b, s: (tile(b, s), 0)),
        out_shape=jax.ShapeDtypeStruct((M, HK), BF16),
        scratch_shapes=[pltpu.VMEM((DN_HEADS, DN_DK, DN_DV), F32),
                        pltpu.VMEM((LANES, C), F32), pltpu.VMEM((LANES, C), F32)],
        compiler_params=_params("arbitrary", "arbitrary"),
        name="deltanet_rev" if reverse else "deltanet_fwd",
    )(p, p, p, p, p, tail, conv_w, alog_row, dtb_row)


def _moe_kernel(be_ref, nu_ref, xs_ref, wg_ref, wu_ref, wd_ref, o_ref, wgb, wub, wdb):
    b = pl.program_id(0)
    e = be_ref[b]
    prev = be_ref[jnp.maximum(b - 1, 0)]
    used = b < nu_ref[0]

    @pl.when(used & ((b == 0) | (e != prev)))
    def _():
        wgb[...] = wg_ref[...].astype(BF16)
        wub[...] = wu_ref[...].astype(BF16)
        wdb[...] = wd_ref[...].astype(BF16)

    @pl.when(used)
    def _():
        x = xs_ref[...].astype(BF16)
        g = jnp.dot(x, wgb[...], preferred_element_type=F32)
        u = jnp.dot(x, wub[...], preferred_element_type=F32)
        a = (_silu(g) * u).astype(BF16)
        o_ref[...] = jnp.dot(a, wdb[...], preferred_element_type=F32)

    @pl.when(jnp.logical_not(used))
    def _():
        o_ref[...] = jnp.zeros_like(o_ref)


def _moe_experts(xs, block_expert, n_used, w_gate, w_up, w_down, layer):
    n_slots, D = xs.shape
    n_blocks = n_slots // MOE_BLOCK
    DE = w_gate.shape[-1]
    grid_spec = pltpu.PrefetchScalarGridSpec(
        num_scalar_prefetch=2,
        grid=(n_blocks,),
        in_specs=[pl.BlockSpec((MOE_BLOCK, D), lambda b, be, nu: (b, 0)),
                  pl.BlockSpec((None, None, D, DE), lambda b, be, nu: (layer, be[b], 0, 0)),
                  pl.BlockSpec((None, None, D, DE), lambda b, be, nu: (layer, be[b], 0, 0)),
                  pl.BlockSpec((None, None, DE, D), lambda b, be, nu: (layer, be[b], 0, 0))],
        out_specs=pl.BlockSpec((MOE_BLOCK, D), lambda b, be, nu: (b, 0)),
        scratch_shapes=[pltpu.VMEM((D, DE), BF16), pltpu.VMEM((D, DE), BF16), pltpu.VMEM((DE, D), BF16)],
    )
    return pl.pallas_call(
        _moe_kernel,
        grid_spec=grid_spec,
        out_shape=jax.ShapeDtypeStruct((n_slots, D), F32),
        compiler_params=_params("arbitrary"),
        name="moe_experts",
    )(block_expert.astype(jnp.int32), n_used.astype(jnp.int32).reshape(1), xs, w_gate, w_up, w_down)


RANK_BLOCK = 128


def _moe(h2, route, w_gate, w_up, w_down, layer):
    N = h2.shape[0]
    NK = N * TOP_K
    assert NK % RANK_BLOCK == 0
    flat_e = route[:, :TOP_K].astype(jnp.int32).reshape(NK)
    onehot = flat_e[:, None] == jnp.arange(N_EXPERTS, dtype=jnp.int32)[None, :]
    oh = onehot.astype(BF16).reshape(NK // RANK_BLOCK, RANK_BLOCK, N_EXPERTS)
    tri = jnp.tril(jnp.ones((RANK_BLOCK, RANK_BLOCK), BF16))
    within = jnp.einsum('ij,bjk->bik', tri, oh, preferred_element_type=F32)
    tot = within[:, -1, :]
    before = jnp.cumsum(tot, axis=0) - tot
    ranks = (within + before[:, None, :]).reshape(NK, N_EXPERTS)
    rank = jnp.sum(jnp.where(onehot, ranks, 0.0), axis=1).astype(jnp.int32) - 1
    counts = (before[-1] + tot[-1]).astype(jnp.int32)
    padded = (counts + MOE_BLOCK - 1) // MOE_BLOCK * MOE_BLOCK
    pad_end = jnp.cumsum(padded)
    pad_start = pad_end - padded
    dest = pad_start[flat_e] + rank
    n_blocks = (NK + N_EXPERTS * (MOE_BLOCK - 1)) // MOE_BLOCK
    n_slots = n_blocks * MOE_BLOCK
    slot_tok = (jnp.arange(n_slots, dtype=jnp.int32) % N).at[dest].set(jnp.arange(NK, dtype=jnp.int32) // TOP_K)
    block_start = jnp.arange(n_blocks, dtype=jnp.int32) * MOE_BLOCK
    block_expert = jnp.minimum(jnp.sum((pad_end[None, :] <= block_start[:, None]).astype(jnp.int32), axis=1),
                               N_EXPERTS - 1)
    xs = h2[slot_tok]
    ys = _moe_experts(xs, block_expert, pad_end[-1] // MOE_BLOCK, w_gate, w_up, w_down, layer)
    dest2 = dest.reshape(N, TOP_K)
    return [ys[dest2[:, k]] for k in range(TOP_K)]


def _tile_mod(mod_rows, n_batch, tiles_per_batch, n_ctx_tiles):
    D = mod_rows.shape[1] // 6
    t = jnp.arange(n_batch * tiles_per_batch)
    src = jnp.where(t % tiles_per_batch < n_ctx_tiles, n_batch, t // tiles_per_batch)
    table = mod_rows.reshape(mod_rows.shape[0], 6, D)[src]
    return jnp.pad(table, ((0, 0), (0, 2), (0, 0)))


def kernel(x, c, ctx, c_ctx, mod_w, mod_b, norm_mix, norm_ffn, lru_w_in, lru_conv_w, lru_conv_b, lru_gate_a_w, lru_gate_a_b, lru_gate_x_w, lru_gate_x_b, lru_lambda, lru_w_out, dn_w_in, dn_conv_w, dn_a_log, dn_dt_bias, dn_norm, dn_w_out, ret_w_in, ret_norm, ret_w_out, att_w_in, att_q_norm, att_k_norm, att_sink, att_w_out, router_w, router_b, moe_w_gate, moe_w_up, moe_w_down):
    Bb, S, D = x.shape
    n_ctx = ctx.shape[1]
    TB = n_ctx + S
    M = Bb * TB
    tiles_per_batch = TB // ROW_TILE
    assert n_ctx % ROW_TILE == 0 and S % ROW_TILE == 0 and Bb < 16

    cond = jnp.zeros((16, D), F32).at[:Bb].set(jax.nn.silu(c)).at[Bb].set(jax.nn.silu(c_ctx)).astype(BF16)
    mod_rows = _adaln_rows(cond, mod_w, mod_b)
    modtiles = [_tile_mod(mod_rows[i], Bb, tiles_per_batch, n_ctx // ROW_TILE) for i in range(DEPTH)]
    router_w_f32 = jnp.pad(router_w.astype(F32), ((0, 0), (0, LANES - N_EXPERTS)))
    router_w_hi = router_w_f32.astype(BF16)
    router_w_pad = jnp.stack([router_w_hi, (router_w_f32 - router_w_hi.astype(F32)).astype(BF16)])
    router_b_pad = jnp.pad(router_b.astype(F32), (0, LANES - N_EXPERTS))[None]
    rope_att = _rope_tables(n_ctx, S, ATT_HD)
    rope_ret = _rope_tables(n_ctx, S, RET_DK)

    def row(width, col=0, dtype_rows=ROW_TILE):
        return pl.BlockSpec((dtype_rows, width), lambda t: (t, col))

    def vec(width):
        return pl.BlockSpec((1, width), lambda t: (0, 0))

    tm_proj = M // 8 if M % (8 * 16) == 0 else 512
    xa, h = _prenorm(ctx, x, norm_mix[0][None], modtiles[0])
    for i in range(DEPTH):
        kind, j = i % N_MIXERS, i // N_MIXERS
        tm = ROW_TILE
        if kind == 0:
            p = _matmul(h, lru_w_in, layer=j, tm=tm_proj)
            cw = jnp.pad(lru_conv_w[j], ((0, SUBLANES - CONV_W), (0, 0)))
            dirs = [_rglru(p, cw, lru_conv_b[j][None], lru_gate_a_w[j, d].astype(BF16), lru_gate_a_b[j, d][None],
                           lru_gate_x_w[j, d].astype(BF16), lru_gate_x_b[j, d][None], lru_lambda[j, d][None],
                           Bb, n_ctx, S, reverse=(d == 1)) for d in range(2)]
            fin = (_finish_lru, [p] + dirs, [row(LRU_WIDTH, 0), row(LRU_WIDTH), row(LRU_WIDTH)])
            w_out = lru_w_out[j]
        elif kind == 1:
            HK = DN_HEADS * DN_DK
            p = _matmul(h, dn_w_in, layer=j, n_out=4 * HK, tm=tm_proj)
            tail = _matmul(h, jnp.pad(dn_w_in[j][:, 4 * HK:], ((0, 0), (0, LANES - 4 * DN_HEADS))), tm=tm_proj)
            cw = jnp.pad(dn_conv_w[j], ((0, SUBLANES - CONV_W), (0, 0)))
            alog_row = jnp.zeros((1, LANES), F32).at[0, 2 * DN_HEADS:4 * DN_HEADS].set(dn_a_log[j].reshape(-1))
            dtb_row = jnp.zeros((1, LANES), F32).at[0, 2 * DN_HEADS:4 * DN_HEADS].set(dn_dt_bias[j].reshape(-1))
            dirs = [_deltanet(p, tail, cw, alog_row, dtb_row, Bb, n_ctx, S, d) for d in range(2)]
            fin = (_finish_dn, [p] + dirs + [jnp.tile(dn_norm[j], DN_HEADS)[None]],
                   [row(HK, 3), row(HK), row(HK), vec(HK)])
            w_out = dn_w_out[j]
        elif kind == 2:
            tm = ROW_TILE // 2
            HV = RET_HEADS * RET_DV
            p = _matmul(h, ret_w_in, layer=j, tm=tm_proj)
            dirs = [_retention(p, rope_ret[0], rope_ret[1], Bb, n_ctx, S, reverse=(d == 1)) for d in range(2)]
            fin = (_finish_ret, [p] + dirs + [ret_norm[j][None]],
                   [row(HV, 2, tm), row(HV, 0, tm), row(HV, 0, tm), vec(HV)])
            w_out = ret_w_out[j]
        else:
            p = _matmul(h, att_w_in, layer=j, tm=tm_proj)
            q, k, v = _att_prep(p, att_q_norm[j][None], att_k_norm[j][None], rope_att[0], rope_att[1],
                                tiles_per_batch)
            o = _attention(q, k, v, att_sink[j], Bb, n_ctx, S)
            fin = (_finish_att, [o], [row(ATT_HEADS * ATT_HD)])
            w_out = att_w_out[j]

        xa, h2, route = _outproj(fin[0], fin[1], fin[2], w_out.astype(BF16), xa, norm_ffn[i][None],
                                 modtiles[i], router_w_pad, router_b_pad, tm)
        ys = _moe(h2, route, moe_w_gate, moe_w_up, moe_w_down, i)
        if i + 1 < DEPTH:
            xa, h = _resid_norm(xa, ys, route, norm_mix[i + 1][None], modtiles[i], modtiles[i + 1])
        else:
            out = _resid_latent(xa, ys, route, modtiles[i], Bb, n_ctx, S)
    return out.reshape(Bb, S, D)
```
